```python
import math
import jax
import jax.numpy as jnp
from jax import lax
import numpy as np

D_MODEL = 2048
BATCH = 4
SEQ = 2048
DEPTH = 4

CHUNK = 64
N_BRANCH = 4
BRANCH = D_MODEL // N_BRANCH
HG_DK = 128
HG_HEADS = BRANCH // HG_DK
HG_DV = BRANCH // HG_HEADS
S5_CH = 16
S5_GROUPS = BRANCH // S5_CH
S5_STATE = 64
M2_HEADDIM = 64
M2_HEADS = BRANCH // M2_HEADDIM
M2_GROUPS = 2
M2_STATE = 128
M2_CONV = 4
M2_XBC = BRANCH + 2 * M2_GROUPS * M2_STATE
RK_HEADSIZE = 64
RK_HEADS = BRANCH // RK_HEADSIZE
RK_DECAY_LORA = 64
RK_A_LORA = 64
RK_GATE_LORA = 128
RK_WIDTH = 3 * BRANCH + RK_DECAY_LORA + RK_A_LORA + RK_GATE_LORA
RK_GN_EPS = 64e-5
GATE_IN = N_BRANCH * D_MODEL
HG_IN = 4 * BRANCH
S5_IN = BRANCH
M2_IN = BRANCH + M2_XBC + M2_HEADS
N_IN = GATE_IN + HG_IN + S5_IN + M2_IN + RK_WIDTH
N_EXPERTS = 32
TOP_K = 4
D_EXPERT = 3 * D_MODEL // 8
SWIGLU_LIMIT = 7.0
SWIGLU_ALPHA = 1.702
MOE_BLOCK = 128
NORM_EPS = 1e-6

kernel_name = 'hybrid_streaming_encoder_block'


def rmsnorm(x, g):
    xf = x.astype(jnp.float32)
    y = xf * lax.rsqrt(jnp.mean(xf * xf, axis=-1, keepdims=True) + NORM_EPS)
    return (y * g.astype(jnp.float32)).astype(x.dtype)


def causal_shift(x):
    return jnp.pad(x, ((0, 0), (1, 0), (0, 0)))[:, :-1]


def to_chunks(t, heads):
    b, s, _ = t.shape
    return t.reshape(b, s // CHUNK, CHUNK, heads, -1).transpose(1, 0, 3, 2, 4)


def hgrn2_mixer(p, lb, onorm):
    b = p.shape[0]
    q, f, i, g = jnp.split(p.astype(jnp.float32), 4, axis=-1)
    lb = lb.astype(jnp.float32)
    forget = lb + (1.0 - lb) * jax.nn.sigmoid(f)
    log_f = jnp.log(forget)
    k = 1.0 - forget
    q = jax.nn.silu(q)
    qc, kc, vc, lc = [to_chunks(t, HG_HEADS) for t in (q, k, i, log_f)]
    mask = jnp.tril(jnp.ones((CHUNK, CHUNK), bool))[:, :, None]

    def step(state, inp):
        qb, kb, vb, lfb = inp
        cum = jnp.cumsum(lfb, axis=2)
        o_inter = jnp.einsum('bhtk,bhkv->bhtv', qb * jnp.exp(cum), state)
        diff = cum[:, :, :, None, :] - cum[:, :, None, :, :]
        decay = jnp.exp(jnp.where(mask, diff, -jnp.inf))
        scores = jnp.einsum('bhtk,bhtsk,bhsk->bhts', qb, decay, kb)
        o = o_inter + jnp.einsum('bhts,bhsv->bhtv', scores, vb)
        last = cum[:, :, -1:, :]
        state = jnp.exp(last[:, :, 0, :])[..., None] * state + jnp.einsum(
            'bhsk,bhsv->bhkv', kb * jnp.exp(last - cum), vb)
        return state, o

    s0 = jnp.zeros((b, HG_HEADS, HG_DK, HG_DV), jnp.float32)
    _, o = lax.scan(step, s0, (qc, kc, vc, lc))
    o = o.transpose(1, 0, 3, 2, 4).reshape(b, -1, HG_HEADS, HG_DV)
    o = rmsnorm(o, onorm).reshape(b, -1, BRANCH)
    return o * jax.nn.silu(g)


def s5_mixer(u, lam_re, lam_im, log_dt, b_re, b_im, c_re, c_im, d_skip, w_glu, b_glu):
    u = u.astype(jnp.float32)
    b, s, _ = u.shape
    ug = u.reshape(b, s, S5_GROUPS, S5_CH)
    lr = lam_re.astype(jnp.float32)
    li = lam_im.astype(jnp.float32)
    dt = jnp.exp(log_dt.astype(jnp.float32))[:, None]
    mag = jnp.exp(lr * dt)
    ab_re = mag * jnp.cos(li * dt)
    ab_im = mag * jnp.sin(li * dt)
    den = lr * lr + li * li
    nr = ab_re - 1.0
    coef_re = (nr * lr + ab_im * li) / den
    coef_im = (ab_im * lr - nr * li) / den
    bu_r = jnp.einsum('gpi,bsgi->bsgp', b_re, ug)
    bu_i = jnp.einsum('gpi,bsgi->bsgp', b_im, ug)
    bu_re = coef_re * bu_r - coef_im * bu_i
    bu_im = coef_re * bu_i + coef_im * bu_r
    a_re = jnp.broadcast_to(ab_re, bu_re.shape)
    a_im = jnp.broadcast_to(ab_im, bu_im.shape)

    def combine(e1, e2):
        a1r, a1i, b1r, b1i = e1
        a2r, a2i, b2r, b2i = e2
        return (a2r * a1r - a2i * a1i,
                a2r * a1i + a2i * a1r,
                a2r * b1r - a2i * b1i + b2r,
                a2r * b1i + a2i * b1r + b2i)

    _, _, x_re, x_im = lax.associative_scan(combine, (a_re, a_im, bu_re, bu_im), axis=1)
    y = jnp.einsum('gip,bsgp->bsgi', c_re, x_re) - jnp.einsum('gip,bsgp->bsgi', c_im, x_im)
    y = y.reshape(b, s, BRANCH) + d_skip * u
    y = jax.nn.gelu(y)
    return y * jax.nn.sigmoid(jnp.einsum('bsw,wv->bsv', y, w_glu) + b_glu)


def causal_depthwise_conv(x, w, bias):
    ch = x.shape[-1]
    y = lax.conv_general_dilated(x, w.astype(x.dtype)[:, None, :], window_strides=(1,),
                                 padding=((M2_CONV - 1, 0),),
                                 dimension_numbers=('NWC', 'WIO', 'NWC'),
                                 feature_group_count=ch)
    return y + bias


def segsum(x):
    t = x.shape[-1]
    xr = jnp.broadcast_to(x[..., :, None], x.shape + (t,))
    xr = jnp.where(jnp.tril(jnp.ones((t, t), bool), -1), xr, 0.0)
    cs = jnp.cumsum(xr, axis=-2)
    return jnp.where(jnp.tril(jnp.ones((t, t), bool)), cs, -jnp.inf)


def ssd_chunked(x, a, bm, cm):
    b, s, h, pd = x.shape
    nc = s // CHUNK
    x = x.reshape(b, nc, CHUNK, h, pd)
    bm = bm.reshape(b, nc, CHUNK, h, -1)
    cm = cm.reshape(b, nc, CHUNK, h, -1)
    a = a.reshape(b, nc, CHUNK, h).transpose(0, 3, 1, 2)
    a_cum = jnp.cumsum(a, axis=-1)
    lmat = jnp.exp(segsum(a))
    cb = jnp.einsum('bclhn,bcshn->bhcls', cm, bm)
    y_diag = jnp.einsum('bhcls,bcshp->bclhp', cb * lmat, x)
    decay_states = jnp.exp(a_cum[..., -1:] - a_cum)
    states = jnp.einsum('bclhn,bhcl,bclhp->bchpn', bm, decay_states, x)
    states = jnp.concatenate([jnp.zeros_like(states[:, :1]), states], axis=1)
    chunk_decay = jnp.exp(segsum(jnp.pad(a_cum[..., -1], ((0, 0), (0, 0), (1, 0)))))
    states = jnp.einsum('bhzc,bchpn->bzhpn', chunk_decay, states)[:, :-1]
    y_off = jnp.einsum('bclhn,bchpn,bhcl->bclhp', cm, states, jnp.exp(a_cum))
    return (y_diag + y_off).reshape(b, s, h, pd)


def mamba2_mixer(p, conv_w, conv_b, dt_bias, a_log, d_skip, norm_g):
    p = p.astype(jnp.float32)
    b, s, _ = p.shape
    z = p[..., :BRANCH]
    xbc = p[..., BRANCH:BRANCH + M2_XBC]
    dt_raw = p[..., BRANCH + M2_XBC:]
    xbc = jax.nn.silu(causal_depthwise_conv(xbc, conv_w, conv_b))
    xs = xbc[..., :BRANCH].reshape(b, s, M2_HEADS, M2_HEADDIM)
    gn = M2_GROUPS * M2_STATE
    rep = M2_HEADS // M2_GROUPS
    bm = jnp.repeat(xbc[..., BRANCH:BRANCH + gn].reshape(b, s, M2_GROUPS, M2_STATE), rep, axis=2)
    cm = jnp.repeat(xbc[..., BRANCH + gn:].reshape(b, s, M2_GROUPS, M2_STATE), rep, axis=2)
    dt = jax.nn.softplus(dt_raw + dt_bias)
    a = -jnp.exp(a_log.astype(jnp.float32))
    y = ssd_chunked(xs * dt[..., None], a * dt, bm, cm) + d_skip[:, None] * xs
    y = y.reshape(b, s, BRANCH) * jax.nn.silu(z)
    y = rmsnorm(y.reshape(b, s, M2_GROUPS, -1), norm_g.reshape(M2_GROUPS, -1))
    return y.reshape(b, s, BRANCH)


def rwkv7_mixer(p, mu, w0, w2, a0, a2, g2, k_k, k_a, r_k, ln_w, ln_b):
    p = p.astype(jnp.float32)
    b, s, _ = p.shape
    pm = p + (causal_shift(p) - p) * mu
    r = pm[..., :BRANCH]
    k = pm[..., BRANCH:2 * BRANCH]
    v = pm[..., 2 * BRANCH:3 * BRANCH]
    o = 3 * BRANCH
    w_lo = pm[..., o:o + RK_DECAY_LORA]
    a_lo = pm[..., o + RK_DECAY_LORA:o + RK_DECAY_LORA + RK_A_LORA]
    g_lo = pm[..., o + RK_DECAY_LORA + RK_A_LORA:]
    w = -jax.nn.softplus(-(w0 + jnp.tanh(w_lo) @ w2)) - 0.5
    decay = jnp.exp(-jnp.exp(w))
    a = jax.nn.sigmoid(a0 + a_lo @ a2)
    g = jax.nn.sigmoid(g_lo) @ g2
    hs = lambda t: t.reshape(b, s, RK_HEADS, RK_HEADSIZE)
    kk = hs(k * k_k)
    kk = kk / jnp.maximum(jnp.linalg.norm(kk, axis=-1, keepdims=True), 1e-12)
    k = k * (1.0 + (a - 1.0) * k_a)
    r_h, k_h, v_h, w_h, a_h = hs(r), hs(k), hs(v), hs(decay), hs(a)
    tm = lambda t: t.transpose(1, 0, 2, 3)

    def step(st, inp):
        w_t, r_t, k_t, v_t, kk_t, a_t = inp
        sa = jnp.einsum('bhvk,bhk->bhv', st, -kk_t)
        st = (st * w_t[:, :, None, :] + sa[..., None] * (kk_t * a_t)[:, :, None, :]
              + v_t[..., None] * k_t[:, :, None, :])
        return st, jnp.einsum('bhvk,bhk->bhv', st, r_t)

    s0 = jnp.zeros((b, RK_HEADS, RK_HEADSIZE, RK_HEADSIZE), jnp.float32)
    _, y = lax.scan(step, s0, (tm(w_h), tm(r_h), tm(k_h), tm(v_h), tm(kk), tm(a_h)))
    y = y.transpose(1, 0, 2, 3)
    mean = jnp.mean(y, axis=-1, keepdims=True)
    var = jnp.mean(jnp.square(y - mean), axis=-1, keepdims=True)
    y = ((y - mean) * lax.rsqrt(var + RK_GN_EPS)).reshape(b, s, BRANCH) * ln_w + ln_b
    bonus = jnp.sum(r_h * k_h * r_k, axis=-1, keepdims=True) * v_h
    return (y + bonus.reshape(b, s, BRANCH)) * g


def mixer_block(h, w_in, w_branch, w_out, hg, s5, m2, rk):
    proj = jnp.einsum('bsd,de->bse', h, w_in)
    o1 = GATE_IN
    o2 = o1 + HG_IN
    o3 = o2 + S5_IN
    o4 = o3 + M2_IN
    ya = hgrn2_mixer(proj[..., o1:o2], *hg)
    yb = s5_mixer(proj[..., o2:o3], *s5)
    yc = mamba2_mixer(proj[..., o3:o4], *m2)
    yd = rwkv7_mixer(proj[..., o4:], *rk)
    ys = jnp.stack([ya, yb, yc, yd], axis=2).astype(h.dtype)
    zs = jnp.einsum('bsnw,nwd->bsnd', ys, w_branch)
    gates = jax.nn.sigmoid(proj[..., :GATE_IN].astype(jnp.float32)).reshape(zs.shape)
    merged = jnp.einsum('bsnd,bsnd->bsd', gates, zs.astype(jnp.float32)).astype(h.dtype)
    return jnp.einsum('bsd,de->bse', merged, w_out)


def moe_ffn(h, w_router, b_router, w_gu, b_gu, w_down, b_down):
    b, s, d = h.shape
    n = b * s
    xt = h.reshape(n, d)
    logits = (xt @ w_router + b_router).astype(jnp.float32)
    top_vals, top_idx = lax.top_k(logits, TOP_K)
    top_w = jax.nn.softmax(top_vals, axis=-1)
    nk = n * TOP_K
    eid = top_idx.reshape(-1)
    tok = jnp.arange(nk, dtype=jnp.int32) // TOP_K
    wts = top_w.reshape(-1)
    order = jnp.argsort(eid, stable=True)
    e_sorted = eid[order]
    counts = jnp.bincount(eid, length=N_EXPERTS)
    padded = ((counts + MOE_BLOCK - 1) // MOE_BLOCK) * MOE_BLOCK
    ends_p = jnp.cumsum(padded)
    starts_p = ends_p - padded
    starts = jnp.cumsum(counts) - counts
    dest = starts_p[e_sorted] + (jnp.arange(nk, dtype=jnp.int32) - starts[e_sorted])
    n_slots = nk + N_EXPERTS * MOE_BLOCK
    n_blocks = n_slots // MOE_BLOCK
    slot_tok = jnp.zeros((n_slots,), jnp.int32).at[dest].set(tok[order])
    slot_w = jnp.zeros((n_slots,), jnp.float32).at[dest].set(wts[order])
    block_e = jnp.minimum(jnp.searchsorted(ends_p, jnp.arange(n_blocks) * MOE_BLOCK, side='right'),
                          N_EXPERTS - 1)
    xs = xt[slot_tok].reshape(n_blocks, MOE_BLOCK, d)

    def expert(args):
        xb, e = args
        gu = xb @ w_gu[e] + b_gu[e]
        gate = jnp.minimum(gu[:, :D_EXPERT], SWIGLU_LIMIT)
        up = jnp.clip(gu[:, D_EXPERT:], -SWIGLU_LIMIT, SWIGLU_LIMIT)
        act = (up + 1.0) * gate * jax.nn.sigmoid(SWIGLU_ALPHA * gate)
        return act @ w_down[e] + b_down[e]

    ys = lax.map(expert, (xs, block_e)).reshape(n_slots, d)
    ys = ys.astype(jnp.float32) * slot_w[:, None]
    out = jax.ops.segment_sum(ys, slot_tok, num_segments=n)
    return out.reshape(b, s, d).astype(h.dtype)


def setup_inputs(seed: int = 0) -> dict:
    key = jax.random.key(seed)
    ks = iter(jax.random.split(key, 64))
    f32 = jnp.float32
    L, D = DEPTH, D_MODEL

    def nrm(shape, scale):
        return scale * jax.random.normal(next(ks), shape, f32)

    def unif(shape, lo, hi):
        return jax.random.uniform(next(ks), shape, f32, lo, hi)

    n_idx = jnp.arange(S5_STATE, dtype=f32)
    dt0 = jnp.exp(unif((L, M2_HEADS), math.log(1e-3), math.log(1e-1)))
    return {
        'x': nrm((BATCH, SEQ, D), 1.0),
        'c': nrm((BATCH, D), 1.0),
        'w_mod': nrm((L, D, 6 * D), 0.5 * D ** -0.5),
        'b_mod': nrm((L, 6 * D), 0.02),
        'g_norm_mix': 1.0 + nrm((L, D), 0.05),
        'g_norm_ffn': 1.0 + nrm((L, D), 0.05),
        'w_in': nrm((L, D, N_IN), D ** -0.5),
        'hg_lower_bound': 1.0 + nrm((L, BRANCH), 0.1),
        'hg_onorm': 1.0 + nrm((L, HG_DV), 0.05),
        's5_lambda_re': -0.5 + nrm((L, S5_GROUPS, S5_STATE), 0.01),
        's5_lambda_im': jnp.pi * n_idx + nrm((L, S5_GROUPS, S5_STATE), 0.01),
        's5_log_dt': unif((L, S5_GROUPS), math.log(1e-3), math.log(1e-1)),
        's5_b_re': nrm((L, S5_GROUPS, S5_STATE, S5_CH), (2 * S5_CH) ** -0.5),
        's5_b_im': nrm((L, S5_GROUPS, S5_STATE, S5_CH), (2 * S5_CH) ** -0.5),
        's5_c_re': nrm((L, S5_GROUPS, S5_CH, S5_STATE), S5_STATE ** -0.5),
        's5_c_im': nrm((L, S5_GROUPS, S5_CH, S5_STATE), S5_STATE ** -0.5),
        's5_d': nrm((L, BRANCH), 1.0),
        's5_w_glu': nrm((L, BRANCH, BRANCH), BRANCH ** -0.5),
        's5_b_glu': nrm((L, BRANCH), 0.02),
        'm2_conv_w': nrm((L, M2_CONV, M2_XBC), M2_CONV ** -0.5),
        'm2_conv_b': nrm((L, M2_XBC), 0.02),
        'm2_dt_bias': dt0 + jnp.log(-jnp.expm1(-dt0)),
        'm2_a_log': jnp.log(unif((L, M2_HEADS), 1.0, 16.0)),
        'm2_d': 1.0 + nrm((L, M2_HEADS), 0.1),
        'm2_norm': 1.0 + nrm((L, BRANCH), 0.05),
        'rk_mu': unif((L, RK_WIDTH), 0.0, 1.0),
        'rk_w0': unif((L, BRANCH), -6.5, -1.5),
        'rk_w2': nrm((L, RK_DECAY_LORA, BRANCH), 0.5 * RK_DECAY_LORA ** -0.5),
        'rk_a0': nrm((L, BRANCH), 0.1),
        'rk_a2': nrm((L, RK_A_LORA, BRANCH), 0.5 * RK_A_LORA ** -0.5),
        'rk_g2': nrm((L, RK_GATE_LORA, BRANCH), RK_GATE_LORA ** -0.5),
        'rk_k_k': 0.85 + nrm((L, BRANCH), 0.02),
        'rk_k_a': 1.0 + nrm((L, BRANCH), 0.02),
        'rk_r_k': nrm((L, RK_HEADS, RK_HEADSIZE), 0.1),
        'rk_ln_w': 1.0 + nrm((L, BRANCH), 0.05),
        'rk_ln_b': nrm((L, BRANCH), 0.02),
        'w_branch': nrm((L, N_BRANCH, BRANCH, D), BRANCH ** -0.5),
        'w_out': nrm((L, D, D), D ** -0.5),
        'w_router': nrm((L, D, N_EXPERTS), D ** -0.5),
        'b_router': nrm((L, N_EXPERTS), 0.01),
        'w_gu': nrm((L, N_EXPERTS, D, 2 * D_EXPERT), D ** -0.5),
        'b_gu': nrm((L, N_EXPERTS, 2 * D_EXPERT), 0.01),
        'w_down': nrm((L, N_EXPERTS, D_EXPERT, D), D_EXPERT ** -0.5),
        'b_down': nrm((L, N_EXPERTS, D), 0.01),
        'g_final': 1.0 + nrm((D,), 0.05),
    }


def reference(x, c, w_mod, b_mod, g_norm_mix, g_norm_ffn, w_in, hg_lower_bound, hg_onorm,
              s5_lambda_re, s5_lambda_im, s5_log_dt, s5_b_re, s5_b_im, s5_c_re, s5_c_im, s5_d,
              s5_w_glu, s5_b_glu, m2_conv_w, m2_conv_b, m2_dt_bias, m2_a_log, m2_d, m2_norm,
              rk_mu, rk_w0, rk_w2, rk_a0, rk_a2, rk_g2, rk_k_k, rk_k_a, rk_r_k, rk_ln_w, rk_ln_b,
              w_branch, w_out, w_router, b_router, w_gu, b_gu, w_down, b_down, g_final):
    lbs = jax.nn.softmax(hg_lower_bound.astype(jnp.float32), axis=0)
    lbs = jnp.cumsum(lbs, axis=0) - lbs[0]
    cond = jax.nn.silu(c)
    for l in range(DEPTH):
        mod = cond @ w_mod[l] + b_mod[l]
        sh_a, sc_a, gt_a, sh_m, sc_m, gt_m = jnp.split(mod[:, None, :], 6, axis=-1)
        h = rmsnorm(x, g_norm_mix[l]) * (1.0 + sc_a) + sh_a
        mix = mixer_block(
            h, w_in[l], w_branch[l], w_out[l],
            (lbs[l], hg_onorm[l]),
            (s5_lambda_re[l], s5_lambda_im[l], s5_log_dt[l], s5_b_re[l], s5_b_im[l],
             s5_c_re[l], s5_c_im[l], s5_d[l], s5_w_glu[l], s5_b_glu[l]),
            (m2_conv_w[l], m2_conv_b[l], m2_dt_bias[l], m2_a_log[l], m2_d[l], m2_norm[l]),
            (rk_mu[l], rk_w0[l], rk_w2[l], rk_a0[l], rk_a2[l], rk_g2[l], rk_k_k[l], rk_k_a[l],
             rk_r_k[l], rk_ln_w[l], rk_ln_b[l]))
        x = x + gt_a * mix
        h = rmsnorm(x, g_norm_ffn[l]) * (1.0 + sc_m) + sh_m
        x = x + gt_m * moe_ffn(h, w_router[l], b_router[l], w_gu[l], b_gu[l], w_down[l], b_down[l])
    return rmsnorm(x, g_final)
```

```python
import functools
import math

import jax
import jax.numpy as jnp
from jax import lax
from jax.experimental import pallas as pl
from jax.experimental.pallas import tpu as pltpu

F32 = jnp.float32
BF16 = jnp.bfloat16
HI = lax.Precision.HIGHEST

D_MODEL = 2048
CHUNK = 64
BRANCH = 512
HG_HEADS, HG_DK = 4, 128
S5_CH, S5_GROUPS, S5_STATE = 16, 32, 64
S5_T = 16
M2_HEADS, M2_HEADDIM, M2_GROUPS, M2_STATE, M2_CONV, M2_XBC = 8, 64, 2, 128, 4, 1024
RK_HEADS, RK_HEADSIZE = 8, 64
RK_DECAY_LORA, RK_A_LORA, RK_GATE_LORA, RK_WIDTH = 64, 64, 128, 1792
RK_GN_EPS = 64e-5
GATE_IN = 4 * D_MODEL
N_MAIN = GATE_IN + 4 * BRANCH + BRANCH + BRANCH + M2_XBC
N_TAIL = RK_WIDTH + 128
N_EXPERTS, TOP_K, D_EXPERT = 32, 4, 768
SWIGLU_LIMIT, SWIGLU_ALPHA = 7.0, 1.702
MOE_ROWS = 256
NORM_EPS = 1e-6
LANES = 128
VMEM_LIMIT = 48 * 1024 * 1024


def _dot(a, b, prec=None):
    return jnp.dot(a, b, preferred_element_type=F32, precision=prec)


def _dot_nt(a, b, prec=None):
    return lax.dot_general(a, b, (((1,), (1,)), ((), ())), preferred_element_type=F32, precision=prec)


def _dot_tn(a, b, prec=None):
    return lax.dot_general(a, b, (((0,), (0,)), ((), ())), preferred_element_type=F32, precision=prec)


def _bdot(a, b):
    return _dot(a.astype(BF16), b.astype(BF16))


def _bdot_nt(a, b):
    return _dot_nt(a.astype(BF16), b.astype(BF16))


def _bdot_tn(a, b):
    return _dot_tn(a.astype(BF16), b.astype(BF16))


def _sigmoid(x):
    return 1.0 / (1.0 + jnp.exp(-x))


def _silu(x):
    return x * _sigmoid(x)


def _softplus(x):
    return jnp.maximum(x, 0.0) + jnp.log(1.0 + jnp.exp(-jnp.abs(x)))


def _tril(n, k=0):
    r = lax.broadcasted_iota(jnp.int32, (n, n), 0)
    c = lax.broadcasted_iota(jnp.int32, (n, n), 1)
    return (c - r) <= k


def _params(*sem):
    return pltpu.CompilerParams(dimension_semantics=sem, vmem_limit_bytes=VMEM_LIMIT)


def _mm_kernel(x_ref, w_ref, o_ref):
    o_ref[...] = _dot(x_ref[...], w_ref[...]).astype(o_ref.dtype)


def _mod_kernel(c_ref, w_ref, b_ref, o_ref):
    o_ref[...] = _dot(_silu(c_ref[...]).astype(BF16), w_ref[...]) + b_ref[...]


def _mm_resid_kernel(x_ref, w_ref, r_ref, g_ref, o_ref):
    o_ref[...] = r_ref[...] + g_ref[...] * _dot(x_ref[...], w_ref[...])


def _matmul(x, w, l, n_cols, tm, tn, out_dtype, name):
    m, k = x.shape
    return pl.pallas_call(
        _mm_kernel,
        grid=(n_cols // tn, m // tm),
        in_specs=[pl.BlockSpec((tm, k), lambda j, i: (i, 0)),
                  pl.BlockSpec((None, k, tn), lambda j, i: (l, 0, j))],
        out_specs=pl.BlockSpec((tm, tn), lambda j, i: (i, j)),
        out_shape=jax.ShapeDtypeStruct((m, n_cols), out_dtype),
        compiler_params=_params("parallel", "parallel"),
        name=name,
    )(x, w)


def _modulation(cond, w_mod, b_mod, l):
    m, k = cond.shape
    n = w_mod.shape[-1]
    tn = 1024
    return pl.pallas_call(
        _mod_kernel,
        grid=(n // tn,),
        in_specs=[pl.BlockSpec((m, k), lambda j: (0, 0)),
                  pl.BlockSpec((None, k, tn), lambda j: (l, 0, j)),
                  pl.BlockSpec((None, 1, tn), lambda j: (l, 0, j))],
        out_specs=pl.BlockSpec((m, tn), lambda j: (0, j)),
        out_shape=jax.ShapeDtypeStruct((m, n), F32),
        compiler_params=_params("parallel"),
        name="modulation",
    )(cond, w_mod, b_mod)


def _out_proj_residual(merged, w_out, l, x, mod4, gate_idx, seq):
    m, k = merged.shape
    tm, tn = 512, 512
    per_b = seq // tm
    return pl.pallas_call(
        _mm_resid_kernel,
        grid=(D_MODEL // tn, m // tm),
        in_specs=[pl.BlockSpec((tm, k), lambda j, i: (i, 0)),
                  pl.BlockSpec((None, k, tn), lambda j, i: (l, 0, j)),
                  pl.BlockSpec((tm, tn), lambda j, i: (i, j)),
                  pl.BlockSpec((None, None, 1, tn), lambda j, i: (i // per_b, gate_idx, 0, j))],
        out_specs=pl.BlockSpec((tm, tn), lambda j, i: (i, j)),
        out_shape=jax.ShapeDtypeStruct((m, D_MODEL), F32),
        compiler_params=_params("parallel", "parallel"),
        name="out_proj_residual",
    )(merged, w_out, x, mod4)


def _rms(x, g):
    return x * lax.rsqrt(jnp.mean(x * x, axis=-1, keepdims=True) + NORM_EPS) * g


def _normmod_kernel(x_ref, g_ref, sc_ref, sh_ref, o_ref):
    y = _rms(x_ref[...], g_ref[...])
    o_ref[...] = (y * (1.0 + sc_ref[...]) + sh_ref[...]).astype(o_ref.dtype)


def _norm_modulate(x, g, l, mod4, shift_idx, scale_idx, seq, out_dtype):
    m, d = x.shape
    ts = 512
    per_b = seq // ts
    return pl.pallas_call(
        _normmod_kernel,
        grid=(m // ts,),
        in_specs=[pl.BlockSpec((ts, d), lambda i: (i, 0)),
                  pl.BlockSpec((None, 1, d), lambda i: (l, 0, 0)),
                  pl.BlockSpec((None, None, 1, d), lambda i: (i // per_b, scale_idx, 0, 0)),
                  pl.BlockSpec((None, None, 1, d), lambda i: (i // per_b, shift_idx, 0, 0))],
        out_specs=pl.BlockSpec((ts, d), lambda i: (i, 0)),
        out_shape=jax.ShapeDtypeStruct((m, d), out_dtype),
        compiler_params=_params("parallel"),
        name="norm_modulate",
    )(x, g, mod4, mod4)


def _final_norm_kernel(x_ref, g_ref, o_ref):
    o_ref[...] = _rms(x_ref[...], g_ref[...])


def _final_norm(x, g):
    m, d = x.shape
    ts = 512
    return pl.pallas_call(
        _final_norm_kernel,
        grid=(m // ts,),
        in_specs=[pl.BlockSpec((ts, d), lambda i: (i, 0)), pl.BlockSpec((1, d), lambda i: (0, 0))],
        out_specs=pl.BlockSpec((ts, d), lambda i: (i, 0)),
        out_shape=jax.ShapeDtypeStruct((m, d), F32),
        compiler_params=_params("parallel"),
        name="final_norm",
    )(x, g)


HG_SUB = 16


def _hgrn2_kernel(q_ref, f_ref, i_ref, g_ref, lb_ref, on_ref, o_ref, state_ref):
    @pl.when(pl.program_id(1) == 0)
    def _():
        state_ref[...] = jnp.zeros_like(state_ref)

    tri = _tril(CHUNK).astype(F32)
    sub_mask = _tril(HG_SUB)
    outs = []
    for h in range(HG_HEADS):
        sl = slice(h * HG_DK, (h + 1) * HG_DK)
        lb = lb_ref[:, sl]
        forget = lb + (1.0 - lb) * _sigmoid(f_ref[:, sl])
        lf = jnp.log(forget)
        k = 1.0 - forget
        q = _silu(q_ref[:, sl])
        v = i_ref[:, sl]
        cum = _dot(tri, lf, HI)
        last = cum[CHUNK - 1:CHUNK, :]
        st = state_ref[h]
        o_inter = _bdot_nt(q * jnp.exp(cum), st)
        rows = []
        for i in range(CHUNK // HG_SUB):
            r0 = i * HG_SUB
            qi = q[r0:r0 + HG_SUB]
            cumi = cum[r0:r0 + HG_SUB]
            oi = o_inter[r0:r0 + HG_SUB]
            if i > 0:
                ci = cum[r0 - 1:r0, :]
                sc = _bdot_nt(qi * jnp.exp(cumi - ci), k[:r0] * jnp.exp(ci - cum[:r0]))
                oi = oi + _bdot(sc, v[:r0])
            for s in range(HG_SUB):
                diff = jnp.where(sub_mask[:, s:s + 1], cumi - cumi[s:s + 1, :], -jnp.inf)
                w = jnp.sum(qi * jnp.exp(diff) * k[r0 + s:r0 + s + 1, :], axis=-1, keepdims=True)
                oi = oi + w * v[r0 + s:r0 + s + 1, :]
            rows.append(oi)
        o = jnp.concatenate(rows, axis=0)
        state_ref[h] = st * jnp.exp(last) + _bdot_tn(v, k * jnp.exp(last - cum))
        o = _rms(o, on_ref[...])
        outs.append(o * _silu(g_ref[:, sl]))
    o_ref[...] = jnp.concatenate(outs, axis=-1)


def _hgrn2(proj, lb, onorm, batch, seq):
    nc = seq // CHUNK
    col0 = GATE_IN // BRANCH
    spec = lambda off: pl.BlockSpec((CHUNK, BRANCH), lambda b, c: (b * nc + c, col0 + off))
    vec = lambda n: pl.BlockSpec((1, n), lambda b, c: (0, 0))
    return pl.pallas_call(
        _hgrn2_kernel,
        grid=(batch, nc),
        in_specs=[spec(0), spec(1), spec(2), spec(3), vec(BRANCH), vec(HG_DK)],
        out_specs=pl.BlockSpec((CHUNK, BRANCH), lambda b, c: (b * nc + c, 0)),
        out_shape=jax.ShapeDtypeStruct((batch * seq, BRANCH), F32),
        scratch_shapes=[pltpu.VMEM((HG_HEADS, HG_DK, HG_DK), F32)],
        compiler_params=_params("parallel", "arbitrary"),
        name="hgrn2",
    )(proj, proj, proj, proj, lb, onorm)


def _mamba2_kernel(z_ref, xbc_ref, dt_ref, cw_ref, cb_ref, dtb_ref, alog_ref, d_ref, ng_ref,
                   o_ref, ext_ref, state_ref):
    @pl.when(pl.program_id(1) == 0)
    def _():
        ext_ref[0:8, :] = jnp.zeros((8, M2_XBC), F32)
        state_ref[...] = jnp.zeros_like(state_ref)

    ext_ref[8:8 + CHUNK, :] = xbc_ref[...]
    conv = cb_ref[...]
    for j in range(M2_CONV):
        conv = conv + cw_ref[j:j + 1, :] * ext_ref[8 - (M2_CONV - 1) + j:8 - (M2_CONV - 1) + j + CHUNK, :]
    ext_ref[0:8, :] = xbc_ref[CHUNK - 8:CHUNK, :]
    xa = _silu(conv)
    gn = M2_GROUPS * M2_STATE
    xs = xa[:, :BRANCH]
    dt = _softplus(dt_ref[...] + dtb_ref[...])
    adt = -jnp.exp(alog_ref[...]) * dt
    tri = _tril(CHUNK)
    cum = _dot(tri.astype(F32), adt, HI)
    cum_t = cum.T
    cb = []
    for g in range(M2_GROUPS):
        bm = xa[:, BRANCH + g * M2_STATE:BRANCH + (g + 1) * M2_STATE]
        cm = xa[:, BRANCH + gn + g * M2_STATE:BRANCH + gn + (g + 1) * M2_STATE]
        cb.append((bm, cm, _bdot_nt(cm, bm)))
    ys = []
    for h in range(M2_HEADS):
        bm, cm, cbg = cb[h // (M2_HEADS // M2_GROUPS)]
        cum_h = cum[:, h:h + 1]
        lmat = jnp.exp(jnp.where(tri, cum_h - cum_t[h:h + 1, :], -jnp.inf))
        x_h = xs[:, h * M2_HEADDIM:(h + 1) * M2_HEADDIM]
        xdt = x_h * dt[:, h:h + 1]
        st = state_ref[h]
        y = _bdot(cbg * lmat, xdt) + jnp.exp(cum_h) * _bdot(cm, st)
        last = cum[CHUNK - 1:CHUNK, h:h + 1]
        state_ref[h] = jnp.exp(last) * st + _bdot_tn(bm * jnp.exp(last - cum_h), xdt)
        ys.append(y)
    y = jnp.concatenate(ys, axis=-1) + d_ref[...] * xs
    y = y * _silu(z_ref[...])
    gw = BRANCH // M2_GROUPS
    outs = [_rms(y[:, g * gw:(g + 1) * gw], ng_ref[:, g * gw:(g + 1) * gw]) for g in range(M2_GROUPS)]
    o_ref[...] = jnp.concatenate(outs, axis=-1)


def _mamba2(proj, tail, conv_w, conv_b, dt_bias, a_log, d_rep, norm_g, batch, seq):
    nc = seq // CHUNK
    zcol = (GATE_IN + 5 * BRANCH) // BRANCH
    xcol = (GATE_IN + 6 * BRANCH) // M2_XBC
    vec = lambda r, n: pl.BlockSpec((r, n), lambda b, c: (0, 0))
    return pl.pallas_call(
        _mamba2_kernel,
        grid=(batch, nc),
        in_specs=[pl.BlockSpec((CHUNK, BRANCH), lambda b, c: (b * nc + c, zcol)),
                  pl.BlockSpec((CHUNK, M2_XBC), lambda b, c: (b * nc + c, xcol)),
                  pl.BlockSpec((CHUNK, LANES), lambda b, c: (b * nc + c, RK_WIDTH // LANES)),
                  vec(M2_CONV, M2_XBC), vec(1, M2_XBC), vec(1, LANES), vec(1, LANES),
                  vec(1, BRANCH), vec(1, BRANCH)],
        out_specs=pl.BlockSpec((CHUNK, BRANCH), lambda b, c: (b * nc + c, 0)),
        out_shape=jax.ShapeDtypeStruct((batch * seq, BRANCH), F32),
        scratch_shapes=[pltpu.VMEM((8 + CHUNK, M2_XBC), F32),
                        pltpu.VMEM((M2_HEADS, M2_STATE, M2_HEADDIM), F32)],
        compiler_params=_params("parallel", "arbitrary"),
        name="mamba2",
    )(proj, proj, tail, conv_w, conv_b, dt_bias, a_log, d_rep, norm_g)


def _unit_lower_inverse(a):
    n = a.shape[0]
    eye = (lax.broadcasted_iota(jnp.int32, (n, n), 0) == lax.broadcasted_iota(jnp.int32, (n, n), 1)).astype(F32)
    inv = eye - a
    p = a
    k = 2
    while k < n:
        p = _dot(p, p, HI)
        inv = _dot(inv, eye + p, HI)
        k *= 2
    return inv


def _rwkv7_kernel(p_ref, mu_ref, w0_ref, w2_ref, a0_ref, a2_ref, g2_ref, kk_ref, ka_ref, rk_ref,
                  lnw_ref, lnb_ref, o_ref, ext_ref, state_ref):
    @pl.when(pl.program_id(1) == 0)
    def _():
        ext_ref[0:8, :] = jnp.zeros((8, RK_WIDTH), F32)
        state_ref[...] = jnp.zeros_like(state_ref)

    p = p_ref[...]
    ext_ref[8:8 + CHUNK, :] = p
    prev = ext_ref[7:7 + CHUNK, :]
    ext_ref[0:8, :] = p_ref[CHUNK - 8:CHUNK, :]
    pm = p + (prev - p) * mu_ref[...]
    b3 = 3 * BRANCH
    r = pm[:, :BRANCH]
    k = pm[:, BRANCH:2 * BRANCH]
    v = pm[:, 2 * BRANCH:b3]
    w_lo = pm[:, b3:b3 + RK_DECAY_LORA]
    a_lo = pm[:, b3 + RK_DECAY_LORA:b3 + RK_DECAY_LORA + RK_A_LORA]
    g_lo = pm[:, b3 + RK_DECAY_LORA + RK_A_LORA:]
    w = -_softplus(-(w0_ref[...] + _bdot(jnp.tanh(w_lo), w2_ref[...]))) - 0.5
    logw = -jnp.exp(w)
    a = _sigmoid(a0_ref[...] + _bdot(a_lo, a2_ref[...]))
    g = _bdot(_sigmoid(g_lo), g2_ref[...])
    kk = k * kk_ref[...]
    k = k * (1.0 + (a - 1.0) * ka_ref[...])
    tri = _tril(CHUNK)
    tri_strict = _tril(CHUNK, -1)
    tri_f = tri.astype(F32)
    outs = []
    for h in range(RK_HEADS):
        sl = slice(h * RK_HEADSIZE, (h + 1) * RK_HEADSIZE)
        kap = kk[:, sl]
        kap = kap / jnp.maximum(jnp.sqrt(jnp.sum(kap * kap, axis=-1, keepdims=True)), 1e-12)
        a_h, k_h, v_h, r_h, lw = a[:, sl], k[:, sl], v[:, sl], r[:, sl], logw[:, sl]
        beta = kap * a_h
        cum = _dot(tri_f, lw, HI)
        last = cum[CHUNK - 1:CHUNK, :]
        ginv = jnp.exp(-cum)
        kq = kap * jnp.exp(cum - lw)
        rq = r_h * jnp.exp(cum)
        bk = beta * ginv
        kd = k_h * ginv
        a_b = jnp.where(tri_strict, _bdot_nt(kq, bk), 0.0)
        a_k = jnp.where(tri_strict, _bdot_nt(kq, kd), 0.0)
        ab_i = jnp.where(tri, _bdot_nt(rq, bk), 0.0)
        ak_i = jnp.where(tri, _bdot_nt(rq, kd), 0.0)
        st = state_ref[h]
        rhs = -(_bdot_nt(kq, st) + _bdot(a_k, v_h))
        u = _dot(_unit_lower_inverse(a_b), rhs, HI)
        y = _bdot_nt(rq, st) + _bdot(ab_i, u) + _bdot(ak_i, v_h)
        to_end = jnp.exp(last - cum)
        state_ref[h] = st * jnp.exp(last) + _bdot_tn(u, beta * to_end) + _bdot_tn(v_h, k_h * to_end)
        mean = jnp.mean(y, axis=-1, keepdims=True)
        yc = y - mean
        var = jnp.mean(yc * yc, axis=-1, keepdims=True)
        yn = yc * lax.rsqrt(var + RK_GN_EPS) * lnw_ref[:, sl] + lnb_ref[:, sl]
        bonus = jnp.sum(r_h * k_h * rk_ref[:, sl], axis=-1, keepdims=True) * v_h
        outs.append(yn + bonus)
    o_ref[...] = jnp.concatenate(outs, axis=-1) * g


def _rwkv7(tail, mu, w0, w2, a0, a2, g2, k_k, k_a, r_k, ln_w, ln_b, batch, seq):
    nc = seq // CHUNK
    vec = lambda r, n: pl.BlockSpec((r, n), lambda b, c: (0, 0))
    return pl.pallas_call(
        _rwkv7_kernel,
        grid=(batch, nc),
        in_specs=[pl.BlockSpec((CHUNK, RK_WIDTH), lambda b, c: (b * nc + c, 0)),
                  vec(1, RK_WIDTH), vec(1, BRANCH), vec(RK_DECAY_LORA, BRANCH), vec(1, BRANCH),
                  vec(RK_A_LORA, BRANCH), vec(RK_GATE_LORA, BRANCH), vec(1, BRANCH), vec(1, BRANCH),
                  vec(1, BRANCH), vec(1, BRANCH), vec(1, BRANCH)],
        out_specs=pl.BlockSpec((CHUNK, BRANCH), lambda b, c: (b * nc + c, 0)),
        out_shape=jax.ShapeDtypeStruct((batch * seq, BRANCH), F32),
        scratch_shapes=[pltpu.VMEM((8 + CHUNK, RK_WIDTH), F32),
                        pltpu.VMEM((RK_HEADS, RK_HEADSIZE, RK_HEADSIZE), F32)],
        compiler_params=_params("parallel", "arbitrary"),
        name="rwkv7",
    )(tail, mu, w0, w2, a0, a2, g2, k_k, k_a, r_k, ln_w, ln_b)


def _s5_tables(lam_re, lam_im, log_dt, b_re, b_im, c_re, c_im, n_chunks):
    t = S5_T
    g, p, i = b_re.shape
    dt = jnp.exp(log_dt)[:, None]

    def apow(n):
        e = n.astype(F32)[..., None, None]
        mag = jnp.exp(e * lam_re * dt)
        return mag * jnp.cos(e * lam_im * dt), mag * jnp.sin(e * lam_im * dt)

    ab_re, ab_im = apow(jnp.ones((), F32))
    den = lam_re * lam_re + lam_im * lam_im
    nr = ab_re - 1.0
    coef_re = ((nr * lam_re + ab_im * lam_im) / den)[..., None]
    coef_im = ((ab_im * lam_re - nr * lam_im) / den)[..., None]
    bt_re = coef_re * b_re - coef_im * b_im
    bt_im = coef_re * b_im + coef_im * b_re
    ar, ai = apow(jnp.arange(t + 1))
    cr, ci = c_re[None], c_im[None]
    ca_re = cr * ar[:, :, None, :] - ci * ai[:, :, None, :]
    ca_im = cr * ai[:, :, None, :] + ci * ar[:, :, None, :]
    kern = (jnp.einsum('tgop,gpi->tgoi', ca_re[:t], bt_re, precision=HI)
            - jnp.einsum('tgop,gpi->tgoi', ca_im[:t], bt_im, precision=HI))
    lag = jnp.arange(t)[None, :] - jnp.arange(t)[:, None]
    toep = jnp.where((lag >= 0)[:, :, None, None, None], kern[jnp.maximum(lag, 0)], 0.0)
    toep = toep.transpose(2, 0, 4, 1, 3).reshape(g, t * i, t * i)
    br, bi = ar[:t][::-1], ai[:t][::-1]
    bs_re = br[..., None] * bt_re - bi[..., None] * bt_im
    bs_im = br[..., None] * bt_im + bi[..., None] * bt_re
    bst = jnp.concatenate([bs_re, bs_im], axis=2).transpose(1, 0, 3, 2).reshape(g, t * i, 2 * p)
    cst = jnp.concatenate([ca_re[1:], -ca_im[1:]], axis=3)
    cst = cst.transpose(1, 3, 0, 2).reshape(g, 2 * p, t * i)
    levels = max(1, int(math.log2(n_chunks)))
    pr, pi = apow(t * (2 ** jnp.arange(levels)))
    m1 = jnp.concatenate([pr, pr], axis=-1)[:, :, None, :]
    m2 = jnp.concatenate([-pi, pi], axis=-1)[:, :, None, :]
    return toep.astype(BF16), bst.astype(BF16), cst.astype(BF16), m1, m2


def _s5_kernel(n_chunks, u_ref, toep_ref, bst_ref, cst_ref, m1_ref, m2_ref, y_ref):
    u = u_ref[...]
    y = _dot(u, toep_ref[...])
    x = _dot(u, bst_ref[...])
    rows = lax.broadcasted_iota(jnp.int32, x.shape, 0) & (n_chunks - 1)
    half = x.shape[1] // 2

    def shifted(val, d):
        return jnp.where(rows >= d, pltpu.roll(val, d, axis=0), 0.0)

    d = 1
    lvl = 0
    while d < n_chunks:
        xs = shifted(x, d)
        x = x + xs * m1_ref[lvl] + pltpu.roll(xs, half, axis=1) * m2_ref[lvl]
        d *= 2
        lvl += 1
    y_ref[...] = y + _dot(shifted(x, 1).astype(BF16), cst_ref[...])


def _s5_scan(u_g, toep, bst, cst, m1, m2, n_chunks):
    g, r, w = u_g.shape
    p2 = bst.shape[-1]
    levels = m1.shape[0]
    return pl.pallas_call(
        functools.partial(_s5_kernel, n_chunks),
        grid=(g,),
        in_specs=[pl.BlockSpec((None, r, w), lambda i: (i, 0, 0)),
                  pl.BlockSpec((None, w, w), lambda i: (i, 0, 0)),
                  pl.BlockSpec((None, w, p2), lambda i: (i, 0, 0)),
                  pl.BlockSpec((None, p2, w), lambda i: (i, 0, 0)),
                  pl.BlockSpec((levels, None, 1, p2), lambda i: (0, i, 0, 0)),
                  pl.BlockSpec((levels, None, 1, p2), lambda i: (0, i, 0, 0))],
        out_specs=pl.BlockSpec((None, r, w), lambda i: (i, 0, 0)),
        out_shape=jax.ShapeDtypeStruct((g, r, w), F32),
        compiler_params=_params("parallel"),
        name="s5_scan",
    )(u_g, toep, bst, cst, m1, m2)


def _s5_post_kernel(y_ref, u_ref, d_ref, w_ref, b_ref, o_ref):
    y = y_ref[...] + d_ref[...] * u_ref[...]
    y = 0.5 * y * (1.0 + jnp.tanh(math.sqrt(2.0 / math.pi) * (y + 0.044715 * (y * y * y))))
    o_ref[...] = y * _sigmoid(_bdot(y, w_ref[...]) + b_ref[...])


def _s5_post(y_ssm, proj, d_skip, w_glu, b_glu):
    m = y_ssm.shape[0]
    tm = min(512, m)
    ucol = (GATE_IN + 4 * BRANCH) // BRANCH
    return pl.pallas_call(
        _s5_post_kernel,
        grid=(m // tm,),
        in_specs=[pl.BlockSpec((tm, BRANCH), lambda i: (i, 0)),
                  pl.BlockSpec((tm, BRANCH), lambda i: (i, ucol)),
                  pl.BlockSpec((1, BRANCH), lambda i: (0, 0)),
                  pl.BlockSpec((BRANCH, BRANCH), lambda i: (0, 0)),
                  pl.BlockSpec((1, BRANCH), lambda i: (0, 0))],
        out_specs=pl.BlockSpec((tm, BRANCH), lambda i: (i, 0)),
        out_shape=jax.ShapeDtypeStruct((m, BRANCH), F32),
        compiler_params=_params("parallel"),
        name="s5_post",
    )(y_ssm, proj, d_skip, w_glu, b_glu)


def _s5(proj, tables, d_skip, w_glu, b_glu, batch, seq):
    n = batch * seq
    nc = seq // S5_T
    u0 = GATE_IN + 4 * BRANCH
    u = proj[:, u0:u0 + BRANCH].reshape(batch, nc, S5_T, S5_GROUPS, S5_CH)
    u_g = u.transpose(3, 0, 1, 2, 4).reshape(S5_GROUPS, batch * nc, S5_T * S5_CH).astype(BF16)
    y_g = _s5_scan(u_g, *tables, nc)
    y = y_g.reshape(S5_GROUPS, batch, nc, S5_T, S5_CH).transpose(1, 2, 3, 0, 4).reshape(n, BRANCH)
    return _s5_post(y, proj, d_skip, w_glu, b_glu)


def _merge_kernel(ya_ref, yb_ref, yc_ref, yd_ref, ga_ref, gb_ref, gc_ref, gd_ref,
                  wa_ref, wb_ref, wc_ref, wd_ref, o_ref):
    acc = None
    for y_ref, g_ref, w_ref in ((ya_ref, ga_ref, wa_ref), (yb_ref, gb_ref, wb_ref),
                                (yc_ref, gc_ref, wc_ref), (yd_ref, gd_ref, wd_ref)):
        term = _sigmoid(g_ref[...]) * _dot(y_ref[...].astype(BF16), w_ref[...])
        acc = term if acc is None else acc + term
    o_ref[...] = acc.astype(o_ref.dtype)


def _merge(ys, proj, w_branch, l):
    m = ys[0].shape[0]
    tm, tn = min(512, m), 512
    per = D_MODEL // tn
    yspec = pl.BlockSpec((tm, BRANCH), lambda j, i: (i, 0))
    gspec = lambda n: pl.BlockSpec((tm, tn), lambda j, i: (i, n * per + j))
    wspec = lambda n: pl.BlockSpec((None, None, BRANCH, tn), lambda j, i: (l, n, 0, j))
    return pl.pallas_call(
        _merge_kernel,
        grid=(D_MODEL // tn, m // tm),
        in_specs=[yspec] * 4 + [gspec(n) for n in range(4)] + [wspec(n) for n in range(4)],
        out_specs=pl.BlockSpec((tm, tn), lambda j, i: (i, j)),
        out_shape=jax.ShapeDtypeStruct((m, D_MODEL), BF16),
        compiler_params=_params("parallel", "parallel"),
        name="branch_merge",
    )(*ys, proj, proj, proj, proj, w_branch, w_branch, w_branch, w_branch)


ROUTER_TM = 256


def _router_kernel(x_ref, g_ref, sc_ref, sh_ref, wr_ref, br_ref,
                   h_ref, idx_ref, wt_ref, rank_ref, cnt_ref, carry_ref):
    @pl.when(pl.program_id(0) == 0)
    def _():
        carry_ref[...] = jnp.zeros_like(carry_ref)

    h = _rms(x_ref[...], g_ref[...]) * (1.0 + sc_ref[...]) + sh_ref[...]
    h_ref[...] = h
    tm = h.shape[0]
    lane = lax.broadcasted_iota(jnp.int32, (tm, LANES), 1)
    logits = _dot(h, wr_ref[...], HI) + br_ref[...]
    masked = jnp.where(lane < N_EXPERTS, logits, -jnp.inf)
    vals, hots, idxs = [], [], []
    for _ in range(TOP_K):
        m = jnp.max(masked, axis=-1, keepdims=True)
        idx = jnp.min(jnp.where(masked == m, lane, LANES), axis=-1, keepdims=True)
        hot = lane == idx
        masked = jnp.where(hot, -jnp.inf, masked)
        vals.append(m)
        hots.append(hot)
        idxs.append(idx)
    exps = [jnp.exp(v - vals[0]) for v in vals]
    denom = exps[0] + exps[1] + exps[2] + exps[3]
    multi = sum(hot.astype(F32) for hot in hots)
    before = _dot(_tril(tm, -1).astype(BF16), multi.astype(BF16)) + carry_ref[...]
    carry_ref[...] = carry_ref[...] + jnp.sum(multi, axis=0, keepdims=True)
    idx_out = jnp.zeros((tm, LANES), jnp.int32)
    wt_out = jnp.zeros((tm, LANES), F32)
    rank_out = jnp.zeros((tm, LANES), F32)
    for k in range(TOP_K):
        rank = jnp.sum(jnp.where(hots[k], before, 0.0), axis=-1, keepdims=True)
        idx_out = jnp.where(lane == k, idxs[k], idx_out)
        wt_out = jnp.where(lane == k, exps[k] / denom, wt_out)
        rank_out = jnp.where(lane == k, rank, rank_out)
    idx_ref[...] = idx_out
    wt_ref[...] = wt_out
    rank_ref[...] = rank_out.astype(jnp.int32)
    cnt_ref[...] = carry_ref[...].astype(jnp.int32)


def _router(x, g, l, mod4, w_router, b_router, seq):
    m, d = x.shape
    tm = min(ROUTER_TM, m)
    per_b = seq // tm
    tok = lambda dt: jax.ShapeDtypeStruct((m, LANES), dt)
    tspec = pl.BlockSpec((tm, LANES), lambda i: (i, 0))
    return pl.pallas_call(
        _router_kernel,
        grid=(m // tm,),
        in_specs=[pl.BlockSpec((tm, d), lambda i: (i, 0)),
                  pl.BlockSpec((None, 1, d), lambda i: (l, 0, 0)),
                  pl.BlockSpec((None, None, 1, d), lambda i: (i // per_b, 4, 0, 0)),
                  pl.BlockSpec((None, None, 1, d), lambda i: (i // per_b, 3, 0, 0)),
                  pl.BlockSpec((None, d, LANES), lambda i: (l, 0, 0)),
                  pl.BlockSpec((None, 1, LANES), lambda i: (l, 0, 0))],
        out_specs=[pl.BlockSpec((tm, d), lambda i: (i, 0)), tspec, tspec, tspec,
                   pl.BlockSpec((1, LANES), lambda i: (0, 0))],
        out_shape=[jax.ShapeDtypeStruct((m, d), F32), tok(jnp.int32), tok(F32), tok(jnp.int32),
                   jax.ShapeDtypeStruct((1, LANES), jnp.int32)],
        scratch_shapes=[pltpu.VMEM((1, LANES), F32)],
        compiler_params=_params("arbitrary"),
        name="router",
    )(x, g, mod4, mod4, w_router, b_router)


def _row_copy(src_hbm, row, buf, slot, r, sem):
    return pltpu.make_async_copy(src_hbm.at[pl.ds(row, 1), :], buf.at[slot, pl.ds(r, 1), :], sem.at[slot])


def _expert_kernel(be_ref, nu_ref, st_ref, h_hbm, wgu_ref, bgu_ref, wd_ref, bd_ref, o_ref, xbuf, sem):
    i = pl.program_id(0)
    n_used = nu_ref[0]

    def issue(blk, slot):
        def body(r, carry):
            _row_copy(h_hbm, st_ref[blk * MOE_ROWS + r], xbuf, slot, r, sem).start()
            return carry
        lax.fori_loop(0, MOE_ROWS, body, 0)

    @pl.when(jnp.logical_and(i == 0, n_used > 0))
    def _():
        issue(0, 0)

    @pl.when(i + 1 < n_used)
    def _():
        issue(i + 1, (i + 1) % 2)

    @pl.when(i < n_used)
    def _():
        slot = i % 2

        def wait(r, carry):
            _row_copy(h_hbm, 0, xbuf, slot, r, sem).wait()
            return carry
        lax.fori_loop(0, MOE_ROWS, wait, 0)
        gu = _dot(xbuf[slot].astype(BF16), wgu_ref[...]) + bgu_ref[...]
        gate = jnp.minimum(gu[:, :D_EXPERT], SWIGLU_LIMIT)
        up = jnp.clip(gu[:, D_EXPERT:], -SWIGLU_LIMIT, SWIGLU_LIMIT)
        act = (up + 1.0) * gate * _sigmoid(SWIGLU_ALPHA * gate)
        o_ref[...] = _dot(act.astype(BF16), wd_ref[...]) + bd_ref[...]

    @pl.when(i >= n_used)
    def _():
        o_ref[...] = jnp.zeros_like(o_ref)


def _experts(h, block_e, n_used, slot_tok, w_gu, b_gu, w_down, b_down, l):
    n_slots = slot_tok.shape[0]
    d = h.shape[1]
    grid_spec = pltpu.PrefetchScalarGridSpec(
        num_scalar_prefetch=3,
        grid=(n_slots // MOE_ROWS,),
        in_specs=[pl.BlockSpec(memory_space=pl.ANY),
                  pl.BlockSpec((None, None, d, 2 * D_EXPERT), lambda i, be, nu, st: (l, be[i], 0, 0)),
                  pl.BlockSpec((None, None, 1, 2 * D_EXPERT), lambda i, be, nu, st: (l, be[i], 0, 0)),
                  pl.BlockSpec((None, None, D_EXPERT, d), lambda i, be, nu, st: (l, be[i], 0, 0)),
                  pl.BlockSpec((None, None, 1, d), lambda i, be, nu, st: (l, be[i], 0, 0))],
        out_specs=pl.BlockSpec((MOE_ROWS, d), lambda i, be, nu, st: (i, 0)),
        scratch_shapes=[pltpu.VMEM((2, MOE_ROWS, d), F32), pltpu.SemaphoreType.DMA((2,))],
    )
    return pl.pallas_call(
        _expert_kernel,
        grid_spec=grid_spec,
        out_shape=jax.ShapeDtypeStruct((n_slots, d), F32),
        compiler_params=_params("arbitrary"),
        name="experts",
    )(block_e, n_used, slot_tok, h, w_gu, b_gu, w_down, b_down)


COMBINE_TM = 128


def _combine_kernel(dest_ref, ys_hbm, wt_ref, x_ref, gt_ref, o_ref, buf, sem):
    i = pl.program_id(0)
    n = pl.num_programs(0)
    tm = COMBINE_TM

    def issue(tile, slot):
        def body(r, carry):
            for k in range(TOP_K):
                row = dest_ref[(tile * tm + r) * TOP_K + k]
                _row_copy(ys_hbm, row, buf, slot * TOP_K + k, r, sem).start()
            return carry
        lax.fori_loop(0, tm, body, 0)

    @pl.when(i == 0)
    def _():
        issue(0, 0)

    @pl.when(i + 1 < n)
    def _():
        issue(i + 1, (i + 1) % 2)

    slot = i % 2

    def wait(r, carry):
        for k in range(TOP_K):
            _row_copy(ys_hbm, 0, buf, slot * TOP_K + k, r, sem).wait()
        return carry
    lax.fori_loop(0, tm, wait, 0)
    acc = None
    for k in range(TOP_K):
        term = wt_ref[:, k:k + 1] * buf[slot * TOP_K + k]
        acc = term if acc is None else acc + term
    o_ref[...] = x_ref[...] + gt_ref[...] * acc


def _combine(dest, ys, wt, x, mod4, seq):
    m, d = x.shape
    tm = COMBINE_TM
    per_b = seq // tm
    grid_spec = pltpu.PrefetchScalarGridSpec(
        num_scalar_prefetch=1,
        grid=(m // tm,),
        in_specs=[pl.BlockSpec(memory_space=pl.ANY),
                  pl.BlockSpec((tm, LANES), lambda i, ds: (i, 0)),
                  pl.BlockSpec((tm, d), lambda i, ds: (i, 0)),
                  pl.BlockSpec((None, None, 1, d), lambda i, ds: (i // per_b, 5, 0, 0))],
        out_specs=pl.BlockSpec((tm, d), lambda i, ds: (i, 0)),
        scratch_shapes=[pltpu.VMEM((2 * TOP_K, tm, d), F32), pltpu.SemaphoreType.DMA((2 * TOP_K,))],
    )
    return pl.pallas_call(
        _combine_kernel,
        grid_spec=grid_spec,
        out_shape=jax.ShapeDtypeStruct((m, d), F32),
        compiler_params=_params("arbitrary"),
        name="moe_combine",
    )(dest, ys, wt, x, mod4)


def _moe(x, g, l, mod4, w_router, b_router, w_gu, b_gu, w_down, b_down, seq):
    m = x.shape[0]
    h, idx, wt, rank, cnt = _router(x, g, l, mod4, w_router, b_router, seq)
    counts = cnt[0, :N_EXPERTS]
    padded = ((counts + MOE_ROWS - 1) // MOE_ROWS) * MOE_ROWS
    ends = jnp.cumsum(padded)
    starts = ends - padded
    dest = (starts[idx[:, :TOP_K]] + rank[:, :TOP_K]).reshape(-1)
    n_slots = m * TOP_K + N_EXPERTS * MOE_ROWS
    n_blocks = n_slots // MOE_ROWS
    slot_tok = jnp.zeros((n_slots,), jnp.int32).at[dest].set(jnp.arange(m * TOP_K, dtype=jnp.int32) // TOP_K)
    block_e = jnp.minimum(jnp.searchsorted(ends, jnp.arange(n_blocks, dtype=jnp.int32) * MOE_ROWS, side='right'),
                          N_EXPERTS - 1).astype(jnp.int32)
    n_used = (ends[-1:] // MOE_ROWS).astype(jnp.int32)
    ys = _experts(h, block_e, n_used, slot_tok, w_gu, b_gu, w_down, b_down, l)
    return _combine(dest, ys, wt, x, mod4, seq)


def _pad_lanes(v):
    return jnp.pad(v, ((0, 0), (0, LANES - v.shape[-1])))[:, None, :]


def kernel(x, c, w_mod, b_mod, g_norm_mix, g_norm_ffn, w_in, hg_lower_bound, hg_onorm, s5_lambda_re, s5_lambda_im, s5_log_dt, s5_b_re, s5_b_im, s5_c_re, s5_c_im, s5_d, s5_w_glu, s5_b_glu, m2_conv_w, m2_conv_b, m2_dt_bias, m2_a_log, m2_d, m2_norm, rk_mu, rk_w0, rk_w2, rk_a0, rk_a2, rk_g2, rk_k_k, rk_k_a, rk_r_k, rk_ln_w, rk_ln_b, w_branch, w_out, w_router, b_router, w_gu, b_gu, w_down, b_down, g_final):
    batch, seq, d = x.shape
    depth = w_in.shape[0]
    n = batch * seq
    w_mod_b = w_mod.astype(BF16)
    w_in_b = w_in.astype(BF16)
    dt0 = N_MAIN
    w_tail = jnp.concatenate([w_in[:, :, dt0 + M2_HEADS:], w_in[:, :, dt0:dt0 + M2_HEADS],
                              jnp.zeros((depth, d, LANES - M2_HEADS), F32)], axis=-1).astype(BF16)
    w_branch_b = w_branch.astype(BF16)
    w_out_b = w_out.astype(BF16)
    w_gu_b = w_gu.astype(BF16)
    w_down_b = w_down.astype(BF16)
    w_router_p = jnp.pad(w_router, ((0, 0), (0, 0), (0, LANES - N_EXPERTS)))
    b_router_p = _pad_lanes(b_router)
    b_gu4 = b_gu[:, :, None, :]
    b_down4 = b_down[:, :, None, :]
    lbs = jax.nn.softmax(hg_lower_bound.astype(F32), axis=0)
    lbs = jnp.cumsum(lbs, axis=0) - lbs[0]
    dt_bias_p = _pad_lanes(m2_dt_bias)
    a_log_p = _pad_lanes(m2_a_log)
    m2_d_rep = jnp.repeat(m2_d, M2_HEADDIM, axis=-1)[:, None, :]
    c_pad = jnp.pad(c, ((0, 8 - batch), (0, 0)))

    x2 = x.reshape(n, d)
    for l in range(depth):
        mod = _modulation(c_pad, w_mod_b, b_mod[:, None, :], l)
        mod4 = mod[:batch].reshape(batch, 6, 1, d)
        h = _norm_modulate(x2, g_norm_mix[:, None, :], l, mod4, 0, 1, seq, BF16)
        proj = _matmul(h, w_in_b, l, N_MAIN, 512, 512, F32, "in_proj_main")
        tail = _matmul(h, w_tail, l, N_TAIL, 512, 640, F32, "in_proj_tail")
        ya = _hgrn2(proj, lbs[l][None], hg_onorm[l][None], batch, seq)
        tables = _s5_tables(s5_lambda_re[l], s5_lambda_im[l], s5_log_dt[l], s5_b_re[l], s5_b_im[l],
                            s5_c_re[l], s5_c_im[l], seq // S5_T)
        yb = _s5(proj, tables, s5_d[l][None], s5_w_glu[l].astype(BF16), s5_b_glu[l][None], batch, seq)
        yc = _mamba2(proj, tail, m2_conv_w[l], m2_conv_b[l][None], dt_bias_p[l], a_log_p[l], m2_d_rep[l],
                     m2_norm[l][None], batch, seq)
        yd = _rwkv7(tail, rk_mu[l][None], rk_w0[l][None], rk_w2[l], rk_a0[l][None], rk_a2[l], rk_g2[l],
                    rk_k_k[l][None], rk_k_a[l][None], rk_r_k[l].reshape(1, BRANCH), rk_ln_w[l][None],
                    rk_ln_b[l][None], batch, seq)
        merged = _merge([ya, yb, yc, yd], proj, w_branch_b, l)
        x2 = _out_proj_residual(merged, w_out_b, l, x2, mod4, 2, seq)
        x2 = _moe(x2, g_norm_ffn[:, None, :], l, mod4, w_router_p, b_router_p, w_gu_b, b_gu4,
                  w_down_b, b_down4, seq)
    return _final_norm(x2, g_final[None]).reshape(batch, seq, d)
```

```python
import functools
import math

import jax
import jax.numpy as jnp
from jax import lax
from jax.experimental import pallas as pl
from jax.experimental.pallas import tpu as pltpu

F32 = jnp.float32
BF16 = jnp.bfloat16
HI = lax.Precision.HIGHEST

D_MODEL = 2048
CHUNK = 64
BRANCH = 512
HG_HEADS, HG_DK = 4, 128
S5_CH, S5_GROUPS, S5_STATE = 16, 32, 64
S5_T = 16
M2_HEADS, M2_HEADDIM, M2_GROUPS, M2_STATE, M2_CONV, M2_XBC = 8, 64, 2, 128, 4, 1024
RK_HEADS, RK_HEADSIZE = 8, 64
RK_DECAY_LORA, RK_A_LORA, RK_GATE_LORA, RK_WIDTH = 64, 64, 128, 1792
RK_GN_EPS = 64e-5
GATE_IN = 4 * D_MODEL
N_MAIN = GATE_IN + 4 * BRANCH + BRANCH + BRANCH + M2_XBC
N_TAIL = RK_WIDTH + 128
N_EXPERTS, TOP_K, D_EXPERT = 32, 4, 768
SWIGLU_LIMIT, SWIGLU_ALPHA = 7.0, 1.702
MOE_ROWS = 256
NORM_EPS = 1e-6
LANES = 128
VMEM_LIMIT = 48 * 1024 * 1024


def _dot(a, b, prec=None):
    return jnp.dot(a, b, preferred_element_type=F32, precision=prec)


def _dot_nt(a, b, prec=None):
    return lax.dot_general(a, b, (((1,), (1,)), ((), ())), preferred_element_type=F32, precision=prec)


def _dot_tn(a, b, prec=None):
    return lax.dot_general(a, b, (((0,), (0,)), ((), ())), preferred_element_type=F32, precision=prec)


def _bdot(a, b):
    return _dot(a.astype(BF16), b.astype(BF16))


def _bdot_nt(a, b):
    return _dot_nt(a.astype(BF16), b.astype(BF16))


def _bdot_tn(a, b):
    return _dot_tn(a.astype(BF16), b.astype(BF16))


def _sigmoid(x):
    return 1.0 / (1.0 + jnp.exp(-x))


def _silu(x):
    return x * _sigmoid(x)


def _softplus(x):
    return jnp.maximum(x, 0.0) + jnp.log(1.0 + jnp.exp(-jnp.abs(x)))


def _tril(n, k=0):
    r = lax.broadcasted_iota(jnp.int32, (n, n), 0)
    c = lax.broadcasted_iota(jnp.int32, (n, n), 1)
    return (c - r) <= k


def _params(*sem):
    return pltpu.CompilerParams(dimension_semantics=sem, vmem_limit_bytes=VMEM_LIMIT)


def _mm_kernel(x_ref, w_ref, o_ref):
    o_ref[...] = _dot(x_ref[...], w_ref[...]).astype(o_ref.dtype)


def _mod_kernel(c_ref, w_ref, b_ref, o_ref):
    o_ref[...] = _dot(_silu(c_ref[...]).astype(BF16), w_ref[...]) + b_ref[...]


def _mm_resid_kernel(x_ref, w_ref, r_ref, g_ref, o_ref):
    o_ref[...] = r_ref[...] + g_ref[...] * _dot(x_ref[...], w_ref[...])


def _matmul(x, w, l, n_cols, tm, tn, out_dtype, name):
    m, k = x.shape
    return pl.pallas_call(
        _mm_kernel,
        grid=(n_cols // tn, m // tm),
        in_specs=[pl.BlockSpec((tm, k), lambda j, i: (i, 0)),
                  pl.BlockSpec((None, k, tn), lambda j, i: (l, 0, j))],
        out_specs=pl.BlockSpec((tm, tn), lambda j, i: (i, j)),
        out_shape=jax.ShapeDtypeStruct((m, n_cols), out_dtype),
        compiler_params=_params("parallel", "parallel"),
        name=name,
    )(x, w)


def _modulation(cond, w_mod, b_mod, l):
    m, k = cond.shape
    n = w_mod.shape[-1]
    tn = 1024
    return pl.pallas_call(
        _mod_kernel,
        grid=(n // tn,),
        in_specs=[pl.BlockSpec((m, k), lambda j: (0, 0)),
                  pl.BlockSpec((None, k, tn), lambda j: (l, 0, j)),
                  pl.BlockSpec((None, 1, tn), lambda j: (l, 0, j))],
        out_specs=pl.BlockSpec((m, tn), lambda j: (0, j)),
        out_shape=jax.ShapeDtypeStruct((m, n), F32),
        compiler_params=_params("parallel"),
        name="modulation",
    )(cond, w_mod, b_mod)


def _out_proj_residual(merged, w_out, l, x, mod4, gate_idx, seq):
    m, k = merged.shape
    tm, tn = 512, 512
    per_b = seq // tm
    return pl.pallas_call(
        _mm_resid_kernel,
        grid=(D_MODEL // tn, m // tm),
        in_specs=[pl.BlockSpec((tm, k), lambda j, i: (i, 0)),
                  pl.BlockSpec((None, k, tn), lambda j, i: (l, 0, j)),
                  pl.BlockSpec((tm, tn), lambda j, i: (i, j)),
                  pl.BlockSpec((None, None, 1, tn), lambda j, i: (i // per_b, gate_idx, 0, j))],
        out_specs=pl.BlockSpec((tm, tn), lambda j, i: (i, j)),
        out_shape=jax.ShapeDtypeStruct((m, D_MODEL), F32),
        compiler_params=_params("parallel", "parallel"),
        name="out_proj_residual",
    )(merged, w_out, x, mod4)


def _rms(x, g):
    return x * lax.rsqrt(jnp.mean(x * x, axis=-1, keepdims=True) + NORM_EPS) * g


def _normmod_kernel(x_ref, g_ref, sc_ref, sh_ref, o_ref):
    y = _rms(x_ref[...], g_ref[...])
    o_ref[...] = (y * (1.0 + sc_ref[...]) + sh_ref[...]).astype(o_ref.dtype)


def _norm_modulate(x, g, l, mod4, shift_idx, scale_idx, seq, out_dtype):
    m, d = x.shape
    ts = 512
    per_b = seq // ts
    return pl.pallas_call(
        _normmod_kernel,
        grid=(m // ts,),
        in_specs=[pl.BlockSpec((ts, d), lambda i: (i, 0)),
                  pl.BlockSpec((None, 1, d), lambda i: (l, 0, 0)),
                  pl.BlockSpec((None, None, 1, d), lambda i: (i // per_b, scale_idx, 0, 0)),
                  pl.BlockSpec((None, None, 1, d), lambda i: (i // per_b, shift_idx, 0, 0))],
        out_specs=pl.BlockSpec((ts, d), lambda i: (i, 0)),
        out_shape=jax.ShapeDtypeStruct((m, d), out_dtype),
        compiler_params=_params("parallel"),
        name="norm_modulate",
    )(x, g, mod4, mod4)


def _final_norm_kernel(x_ref, g_ref, o_ref):
    o_ref[...] = _rms(x_ref[...], g_ref[...])


def _final_norm(x, g):
    m, d = x.shape
    ts = 512
    return pl.pallas_call(
        _final_norm_kernel,
        grid=(m // ts,),
        in_specs=[pl.BlockSpec((ts, d), lambda i: (i, 0)), pl.BlockSpec((1, d), lambda i: (0, 0))],
        out_specs=pl.BlockSpec((ts, d), lambda i: (i, 0)),
        out_shape=jax.ShapeDtypeStruct((m, d), F32),
        compiler_params=_params("parallel"),
        name="final_norm",
    )(x, g)


HG_SUB = 16


def _hgrn2_kernel(q_ref, f_ref, i_ref, g_ref, lb_ref, on_ref, o_ref, state_ref):
    @pl.when(pl.program_id(1) == 0)
    def _():
        state_ref[...] = jnp.zeros_like(state_ref)

    tri = _tril(CHUNK).astype(F32)
    sub_mask = _tril(HG_SUB)
    outs = []
    for h in range(HG_HEADS):
        sl = slice(h * HG_DK, (h + 1) * HG_DK)
        lb = lb_ref[:, sl]
        forget = lb + (1.0 - lb) * _sigmoid(f_ref[:, sl])
        lf = jnp.log(forget)
        k = 1.0 - forget
        q = _silu(q_ref[:, sl])
        v = i_ref[:, sl]
        cum = _dot(tri, lf, HI)
        last = cum[CHUNK - 1:CHUNK, :]
        st = state_ref[h]
        o_inter = _bdot_nt(q * jnp.exp(cum), st)
        rows = []
        for i in range(CHUNK // HG_SUB):
            r0 = i * HG_SUB
            qi = q[r0:r0 + HG_SUB]
            cumi = cum[r0:r0 + HG_SUB]
            oi = o_inter[r0:r0 + HG_SUB]
            if i > 0:
                ci = cum[r0 - 1:r0, :]
                sc = _bdot_nt(qi * jnp.exp(cumi - ci), k[:r0] * jnp.exp(ci - cum[:r0]))
                oi = oi + _bdot(sc, v[:r0])
            for s in range(HG_SUB):
                diff = jnp.where(sub_mask[:, s:s + 1], cumi - cumi[s:s + 1, :], -jnp.inf)
                w = jnp.sum(qi * jnp.exp(diff) * k[r0 + s:r0 + s + 1, :], axis=-1, keepdims=True)
                oi = oi + w * v[r0 + s:r0 + s + 1, :]
            rows.append(oi)
        o = jnp.concatenate(rows, axis=0)
        state_ref[h] = st * jnp.exp(last) + _bdot_tn(v, k * jnp.exp(last - cum))
        o = _rms(o, on_ref[...])
        outs.append(o * _silu(g_ref[:, sl]))
    o_ref[...] = jnp.concatenate(outs, axis=-1)


def _hgrn2(proj, lb, onorm, batch, seq):
    nc = seq // CHUNK
    col0 = GATE_IN // BRANCH
    spec = lambda off: pl.BlockSpec((CHUNK, BRANCH), lambda b, c: (b * nc + c, col0 + off))
    vec = lambda n: pl.BlockSpec((1, n), lambda b, c: (0, 0))
    return pl.pallas_call(
        _hgrn2_kernel,
        grid=(batch, nc),
        in_specs=[spec(0), spec(1), spec(2), spec(3), vec(BRANCH), vec(HG_DK)],
        out_specs=pl.BlockSpec((CHUNK, BRANCH), lambda b, c: (b * nc + c, 0)),
        out_shape=jax.ShapeDtypeStruct((batch * seq, BRANCH), F32),
        scratch_shapes=[pltpu.VMEM((HG_HEADS, HG_DK, HG_DK), F32)],
        compiler_params=_params("parallel", "arbitrary"),
        name="hgrn2",
    )(proj, proj, proj, proj, lb, onorm)


def _mamba2_kernel(z_ref, xbc_ref, dt_ref, cw_ref, cb_ref, dtb_ref, alog_ref, d_ref, ng_ref,
                   o_ref, ext_ref, state_ref):
    @pl.when(pl.program_id(1) == 0)
    def _():
        ext_ref[0:8, :] = jnp.zeros((8, M2_XBC), F32)
        state_ref[...] = jnp.zeros_like(state_ref)

    ext_ref[8:8 + CHUNK, :] = xbc_ref[...]
    conv = cb_ref[...]
    for j in range(M2_CONV):
        conv = conv + cw_ref[j:j + 1, :] * ext_ref[8 - (M2_CONV - 1) + j:8 - (M2_CONV - 1) + j + CHUNK, :]
    ext_ref[0:8, :] = xbc_ref[CHUNK - 8:CHUNK, :]
    xa = _silu(conv)
    gn = M2_GROUPS * M2_STATE
    xs = xa[:, :BRANCH]
    dt = _softplus(dt_ref[...] + dtb_ref[...])
    adt = -jnp.exp(alog_ref[...]) * dt
    tri = _tril(CHUNK)
    cum = _dot(tri.astype(F32), adt, HI)
    cum_t = cum.T
    cb = []
    for g in range(M2_GROUPS):
        bm = xa[:, BRANCH + g * M2_STATE:BRANCH + (g + 1) * M2_STATE]
        cm = xa[:, BRANCH + gn + g * M2_STATE:BRANCH + gn + (g + 1) * M2_STATE]
        cb.append((bm, cm, _bdot_nt(cm, bm)))
    ys = []
    for h in range(M2_HEADS):
        bm, cm, cbg = cb[h // (M2_HEADS // M2_GROUPS)]
        cum_h = cum[:, h:h + 1]
        lmat = jnp.exp(jnp.where(tri, cum_h - cum_t[h:h + 1, :], -jnp.inf))
        x_h = xs[:, h * M2_HEADDIM:(h + 1) * M2_HEADDIM]
        xdt = x_h * dt[:, h:h + 1]
        st = state_ref[h]
        y = _bdot(cbg * lmat, xdt) + jnp.exp(cum_h) * _bdot(cm, st)
        last = cum[CHUNK - 1:CHUNK, h:h + 1]
        state_ref[h] = jnp.exp(last) * st + _bdot_tn(bm * jnp.exp(last - cum_h), xdt)
        ys.append(y)
    y = jnp.concatenate(ys, axis=-1) + d_ref[...] * xs
    y = y * _silu(z_ref[...])
    gw = BRANCH // M2_GROUPS
    outs = [_rms(y[:, g * gw:(g + 1) * gw], ng_ref[:, g * gw:(g + 1) * gw]) for g in range(M2_GROUPS)]
    o_ref[...] = jnp.concatenate(outs, axis=-1)


def _mamba2(proj, tail, conv_w, conv_b, dt_bias, a_log, d_rep, norm_g, batch, seq):
    nc = seq // CHUNK
    zcol = (GATE_IN + 5 * BRANCH) // BRANCH
    xcol = (GATE_IN + 6 * BRANCH) // M2_XBC
    vec = lambda r, n: pl.BlockSpec((r, n), lambda b, c: (0, 0))
    return pl.pallas_call(
        _mamba2_kernel,
        grid=(batch, nc),
        in_specs=[pl.BlockSpec((CHUNK, BRANCH), lambda b, c: (b * nc + c, zcol)),
                  pl.BlockSpec((CHUNK, M2_XBC), lambda b, c: (b * nc + c, xcol)),
                  pl.BlockSpec((CHUNK, LANES), lambda b, c: (b * nc + c, RK_WIDTH // LANES)),
                  vec(M2_CONV, M2_XBC), vec(1, M2_XBC), vec(1, LANES), vec(1, LANES),
                  vec(1, BRANCH), vec(1, BRANCH)],
        out_specs=pl.BlockSpec((CHUNK, BRANCH), lambda b, c: (b * nc + c, 0)),
        out_shape=jax.ShapeDtypeStruct((batch * seq, BRANCH), F32),
        scratch_shapes=[pltpu.VMEM((8 + CHUNK, M2_XBC), F32),
                        pltpu.VMEM((M2_HEADS, M2_STATE, M2_HEADDIM), F32)],
        compiler_params=_params("parallel", "arbitrary"),
        name="mamba2",
    )(proj, proj, tail, conv_w, conv_b, dt_bias, a_log, d_rep, norm_g)


def _split_bf16(x):
    hi = x.astype(BF16)
    return hi, (x - hi.astype(F32)).astype(BF16)


def _dot3(a, b):
    ah, al = _split_bf16(a)
    bh, bl = _split_bf16(b)
    return _dot(ah, bh) + (_dot(ah, bl) + _dot(al, bh))


def _unit_lower_inverses(mats):
    n = mats[0].shape[0]
    eye = (lax.broadcasted_iota(jnp.int32, (n, n), 0) == lax.broadcasted_iota(jnp.int32, (n, n), 1)).astype(F32)
    invs = [eye - a for a in mats]
    pows = list(mats)
    k = 2
    while k < n:
        pows = [_dot3(p, p) for p in pows]
        invs = [_dot3(inv, eye + p) for inv, p in zip(invs, pows)]
        k *= 2
    return invs


def _rwkv7_kernel(p_ref, mu_ref, w0_ref, w2_ref, a0_ref, a2_ref, g2_ref, kk_ref, ka_ref, rk_ref,
                  lnw_ref, lnb_ref, o_ref, ext_ref, state_ref):
    @pl.when(pl.program_id(1) == 0)
    def _():
        ext_ref[0:8, :] = jnp.zeros((8, RK_WIDTH), F32)
        state_ref[...] = jnp.zeros_like(state_ref)

    p = p_ref[...]
    ext_ref[8:8 + CHUNK, :] = p
    prev = ext_ref[7:7 + CHUNK, :]
    ext_ref[0:8, :] = p_ref[CHUNK - 8:CHUNK, :]
    pm = p + (prev - p) * mu_ref[...]
    b3 = 3 * BRANCH
    r = pm[:, :BRANCH]
    k = pm[:, BRANCH:2 * BRANCH]
    v = pm[:, 2 * BRANCH:b3]
    w_lo = pm[:, b3:b3 + RK_DECAY_LORA]
    a_lo = pm[:, b3 + RK_DECAY_LORA:b3 + RK_DECAY_LORA + RK_A_LORA]
    g_lo = pm[:, b3 + RK_DECAY_LORA + RK_A_LORA:]
    w = -_softplus(-(w0_ref[...] + _bdot(jnp.tanh(w_lo), w2_ref[...]))) - 0.5
    logw = -jnp.exp(w)
    a = _sigmoid(a0_ref[...] + _bdot(a_lo, a2_ref[...]))
    g = _bdot(_sigmoid(g_lo), g2_ref[...])
    kk = k * kk_ref[...]
    k = k * (1.0 + (a - 1.0) * ka_ref[...])
    c, hs = CHUNK, RK_HEADSIZE
    row = lax.broadcasted_iota(jnp.int32, (2 * c, 2 * c), 0)
    col = lax.broadcasted_iota(jnp.int32, (2 * c, 2 * c), 1)
    quad_mask = (col & (c - 1)) < (row & (c - 1)) + (row >= c).astype(jnp.int32)
    cum = _dot(_tril(c).astype(F32), logw, HI)
    last = cum[c - 1:c, :]
    gam = jnp.exp(cum)
    ginv = jnp.exp(-cum)
    to_end = jnp.exp(last - cum)
    gam_prev = jnp.exp(cum - logw)
    heads = range(RK_HEADS)
    sls = [slice(h * hs, (h + 1) * hs) for h in heads]
    kap = []
    for sl in sls:
        kh = kk[:, sl]
        kap.append(kh / jnp.maximum(jnp.sqrt(jnp.sum(kh * kh, axis=-1, keepdims=True)), 1e-12))
    beta = [kap[h] * a[:, sls[h]] for h in heads]
    lhs = [jnp.concatenate([kap[h] * gam_prev[:, sls[h]], r[:, sls[h]] * gam[:, sls[h]]], axis=0)
           for h in heads]
    rhs = [jnp.concatenate([beta[h] * ginv[:, sls[h]], k[:, sls[h]] * ginv[:, sls[h]]], axis=0)
           for h in heads]
    quad = [jnp.where(quad_mask, _bdot_nt(lhs[h], rhs[h]), 0.0) for h in heads]
    invs = _unit_lower_inverses([q[:c, :c] for q in quad])
    vs = [v[:, sl] for sl in sls]
    akv = [_bdot(quad[h][:, c:], vs[h]) for h in heads]
    sts = [state_ref[h] for h in heads]
    from_state = [_bdot_nt(lhs[h], sts[h]) for h in heads]
    us = [_dot3(invs[h], -(from_state[h][:c] + akv[h][:c])) for h in heads]
    ys = [from_state[h][c:] + akv[h][c:] + _bdot(quad[h][c:, :c], us[h]) for h in heads]
    for h in heads:
        sl = sls[h]
        upd = _bdot_tn(jnp.concatenate([us[h], vs[h]], axis=0),
                       jnp.concatenate([beta[h] * to_end[:, sl], k[:, sl] * to_end[:, sl]], axis=0))
        state_ref[h] = sts[h] * gam[c - 1:c, sl] + upd
    outs = []
    for h in heads:
        sl = sls[h]
        mean = jnp.mean(ys[h], axis=-1, keepdims=True)
        yc = ys[h] - mean
        var = jnp.mean(yc * yc, axis=-1, keepdims=True)
        yn = yc * lax.rsqrt(var + RK_GN_EPS) * lnw_ref[:, sl] + lnb_ref[:, sl]
        bonus = jnp.sum(r[:, sl] * k[:, sl] * rk_ref[:, sl], axis=-1, keepdims=True) * vs[h]
        outs.append(yn + bonus)
    o_ref[...] = jnp.concatenate(outs, axis=-1) * g


def _rwkv7(tail, mu, w0, w2, a0, a2, g2, k_k, k_a, r_k, ln_w, ln_b, batch, seq):
    nc = seq // CHUNK
    vec = lambda r, n: pl.BlockSpec((r, n), lambda b, c: (0, 0))
    return pl.pallas_call(
        _rwkv7_kernel,
        grid=(batch, nc),
        in_specs=[pl.BlockSpec((CHUNK, RK_WIDTH), lambda b, c: (b * nc + c, 0)),
                  vec(1, RK_WIDTH), vec(1, BRANCH), vec(RK_DECAY_LORA, BRANCH), vec(1, BRANCH),
                  vec(RK_A_LORA, BRANCH), vec(RK_GATE_LORA, BRANCH), vec(1, BRANCH), vec(1, BRANCH),
                  vec(1, BRANCH), vec(1, BRANCH), vec(1, BRANCH)],
        out_specs=pl.BlockSpec((CHUNK, BRANCH), lambda b, c: (b * nc + c, 0)),
        out_shape=jax.ShapeDtypeStruct((batch * seq, BRANCH), F32),
        scratch_shapes=[pltpu.VMEM((8 + CHUNK, RK_WIDTH), F32),
                        pltpu.VMEM((RK_HEADS, RK_HEADSIZE, RK_HEADSIZE), F32)],
        compiler_params=_params("parallel", "arbitrary"),
        name="rwkv7",
    )(tail, mu, w0, w2, a0, a2, g2, k_k, k_a, r_k, ln_w, ln_b)


def _s5_tables(lam_re, lam_im, log_dt, b_re, b_im, c_re, c_im, n_chunks):
    t = S5_T
    g, p, i = b_re.shape
    dt = jnp.exp(log_dt)[:, None]

    def apow(n):
        e = n.astype(F32)[..., None, None]
        mag = jnp.exp(e * lam_re * dt)
        return mag * jnp.cos(e * lam_im * dt), mag * jnp.sin(e * lam_im * dt)

    ab_re, ab_im = apow(jnp.ones((), F32))
    den = lam_re * lam_re + lam_im * lam_im
    nr = ab_re - 1.0
    coef_re = ((nr * lam_re + ab_im * lam_im) / den)[..., None]
    coef_im = ((ab_im * lam_re - nr * lam_im) / den)[..., None]
    bt_re = coef_re * b_re - coef_im * b_im
    bt_im = coef_re * b_im + coef_im * b_re
    ar, ai = apow(jnp.arange(t + 1))
    cr, ci = c_re[None], c_im[None]
    ca_re = cr * ar[:, :, None, :] - ci * ai[:, :, None, :]
    ca_im = cr * ai[:, :, None, :] + ci * ar[:, :, None, :]
    kern = (jnp.einsum('tgop,gpi->tgoi', ca_re[:t], bt_re, precision=HI)
            - jnp.einsum('tgop,gpi->tgoi', ca_im[:t], bt_im, precision=HI))
    lag = jnp.arange(t)[None, :] - jnp.arange(t)[:, None]
    toep = jnp.where((lag >= 0)[:, :, None, None, None], kern[jnp.maximum(lag, 0)], 0.0)
    toep = toep.transpose(2, 0, 4, 1, 3).reshape(g, t * i, t * i)
    br, bi = ar[:t][::-1], ai[:t][::-1]
    bs_re = br[..., None] * bt_re - bi[..., None] * bt_im
    bs_im = br[..., None] * bt_im + bi[..., None] * bt_re
    bst = jnp.concatenate([bs_re, bs_im], axis=2).transpose(1, 0, 3, 2).reshape(g, t * i, 2 * p)
    cst = jnp.concatenate([ca_re[1:], -ca_im[1:]], axis=3)
    cst = cst.transpose(1, 3, 0, 2).reshape(g, 2 * p, t * i)
    levels = max(1, int(math.log2(n_chunks)))
    pr, pi = apow(t * (2 ** jnp.arange(levels)))
    m1 = jnp.concatenate([pr, pr], axis=-1)[:, :, None, :]
    m2 = jnp.concatenate([-pi, pi], axis=-1)[:, :, None, :]
    return toep.astype(BF16), bst.astype(BF16), cst.astype(BF16), m1, m2


def _s5_kernel(n_chunks, u_ref, toep_ref, bst_ref, cst_ref, m1_ref, m2_ref, y_ref):
    u = u_ref[...]
    y = _dot(u, toep_ref[...])
    x = _dot(u, bst_ref[...])
    rows = lax.broadcasted_iota(jnp.int32, x.shape, 0) & (n_chunks - 1)
    half = x.shape[1] // 2

    def shifted(val, d):
        return jnp.where(rows >= d, pltpu.roll(val, d, axis=0), 0.0)

    d = 1
    lvl = 0
    while d < n_chunks:
        xs = shifted(x, d)
        x = x + xs * m1_ref[lvl] + pltpu.roll(xs, half, axis=1) * m2_ref[lvl]
        d *= 2
        lvl += 1
    y_ref[...] = y + _dot(shifted(x, 1).astype(BF16), cst_ref[...])


def _s5_scan(u_g, toep, bst, cst, m1, m2, n_chunks):
    g, r, w = u_g.shape
    p2 = bst.shape[-1]
    levels = m1.shape[0]
    return pl.pallas_call(
        functools.partial(_s5_kernel, n_chunks),
        grid=(g,),
        in_specs=[pl.BlockSpec((None, r, w), lambda i: (i, 0, 0)),
                  pl.BlockSpec((None, w, w), lambda i: (i, 0, 0)),
                  pl.BlockSpec((None, w, p2), lambda i: (i, 0, 0)),
                  pl.BlockSpec((None, p2, w), lambda i: (i, 0, 0)),
                  pl.BlockSpec((levels, None, 1, p2), lambda i: (0, i, 0, 0)),
                  pl.BlockSpec((levels, None, 1, p2), lambda i: (0, i, 0, 0))],
        out_specs=pl.BlockSpec((None, r, w), lambda i: (i, 0, 0)),
        out_shape=jax.ShapeDtypeStruct((g, r, w), F32),
        compiler_params=_params("parallel"),
        name="s5_scan",
    )(u_g, toep, bst, cst, m1, m2)


def _s5_post_kernel(y_ref, u_ref, d_ref, w_ref, b_ref, o_ref):
    y = y_ref[...] + d_ref[...] * u_ref[...]
    y = 0.5 * y * (1.0 + jnp.tanh(math.sqrt(2.0 / math.pi) * (y + 0.044715 * (y * y * y))))
    o_ref[...] = y * _sigmoid(_bdot(y, w_ref[...]) + b_ref[...])


def _s5_post(y_ssm, proj, d_skip, w_glu, b_glu):
    m = y_ssm.shape[0]
    tm = min(512, m)
    ucol = (GATE_IN + 4 * BRANCH) // BRANCH
    return pl.pallas_call(
        _s5_post_kernel,
        grid=(m // tm,),
        in_specs=[pl.BlockSpec((tm, BRANCH), lambda i: (i, 0)),
                  pl.BlockSpec((tm, BRANCH), lambda i: (i, ucol)),
                  pl.BlockSpec((1, BRANCH), lambda i: (0, 0)),
                  pl.BlockSpec((BRANCH, BRANCH), lambda i: (0, 0)),
                  pl.BlockSpec((1, BRANCH), lambda i: (0, 0))],
        out_specs=pl.BlockSpec((tm, BRANCH), lambda i: (i, 0)),
        out_shape=jax.ShapeDtypeStruct((m, BRANCH), F32),
        compiler_params=_params("parallel"),
        name="s5_post",
    )(y_ssm, proj, d_skip, w_glu, b_glu)


def _s5(proj, tables, d_skip, w_glu, b_glu, batch, seq):
    n = batch * seq
    nc = seq // S5_T
    u0 = GATE_IN + 4 * BRANCH
    u = proj[:, u0:u0 + BRANCH].reshape(batch, nc, S5_T, S5_GROUPS, S5_CH)
    u_g = u.transpose(3, 0, 1, 2, 4).reshape(S5_GROUPS, batch * nc, S5_T * S5_CH).astype(BF16)
    y_g = _s5_scan(u_g, *tables, nc)
    y = y_g.reshape(S5_GROUPS, batch, nc, S5_T, S5_CH).transpose(1, 2, 3, 0, 4).reshape(n, BRANCH)
    return _s5_post(y, proj, d_skip, w_glu, b_glu)


def _merge_kernel(ya_ref, yb_ref, yc_ref, yd_ref, ga_ref, gb_ref, gc_ref, gd_ref,
                  wa_ref, wb_ref, wc_ref, wd_ref, o_ref):
    acc = None
    for y_ref, g_ref, w_ref in ((ya_ref, ga_ref, wa_ref), (yb_ref, gb_ref, wb_ref),
                                (yc_ref, gc_ref, wc_ref), (yd_ref, gd_ref, wd_ref)):
        term = _sigmoid(g_ref[...]) * _dot(y_ref[...].astype(BF16), w_ref[...])
        acc = term if acc is None else acc + term
    o_ref[...] = acc.astype(o_ref.dtype)


def _merge(ys, proj, w_branch, l):
    m = ys[0].shape[0]
    tm, tn = min(512, m), 512
    per = D_MODEL // tn
    yspec = pl.BlockSpec((tm, BRANCH), lambda j, i: (i, 0))
    gspec = lambda n: pl.BlockSpec((tm, tn), lambda j, i: (i, n * per + j))
    wspec = lambda n: pl.BlockSpec((None, None, BRANCH, tn), lambda j, i: (l, n, 0, j))
    return pl.pallas_call(
        _merge_kernel,
        grid=(D_MODEL // tn, m // tm),
        in_specs=[yspec] * 4 + [gspec(n) for n in range(4)] + [wspec(n) for n in range(4)],
        out_specs=pl.BlockSpec((tm, tn), lambda j, i: (i, j)),
        out_shape=jax.ShapeDtypeStruct((m, D_MODEL), BF16),
        compiler_params=_params("parallel", "parallel"),
        name="branch_merge",
    )(*ys, proj, proj, proj, proj, w_branch, w_branch, w_branch, w_branch)


ROUTER_TM = 256


def _router_kernel(x_ref, g_ref, sc_ref, sh_ref, wr_ref, br_ref,
                   h_ref, idx_ref, wt_ref, rank_ref, cnt_ref, carry_ref):
    @pl.when(pl.program_id(0) == 0)
    def _():
        carry_ref[...] = jnp.zeros_like(carry_ref)

    h = _rms(x_ref[...], g_ref[...]) * (1.0 + sc_ref[...]) + sh_ref[...]
    h_ref[...] = h
    tm = h.shape[0]
    lane = lax.broadcasted_iota(jnp.int32, (tm, LANES), 1)
    logits = _dot(h, wr_ref[...], HI) + br_ref[...]
    masked = jnp.where(lane < N_EXPERTS, logits, -jnp.inf)
    vals, hots, idxs = [], [], []
    for _ in range(TOP_K):
        m = jnp.max(masked, axis=-1, keepdims=True)
        idx = jnp.min(jnp.where(masked == m, lane, LANES), axis=-1, keepdims=True)
        hot = lane == idx
        masked = jnp.where(hot, -jnp.inf, masked)
        vals.append(m)
        hots.append(hot)
        idxs.append(idx)
    exps = [jnp.exp(v - vals[0]) for v in vals]
    denom = exps[0] + exps[1] + exps[2] + exps[3]
    multi = sum(hot.astype(F32) for hot in hots)
    before = _dot(_tril(tm, -1).astype(BF16), multi.astype(BF16)) + carry_ref[...]
    carry_ref[...] = carry_ref[...] + jnp.sum(multi, axis=0, keepdims=True)
    idx_out = jnp.zeros((tm, LANES), jnp.int32)
    wt_out = jnp.zeros((tm, LANES), F32)
    rank_out = jnp.zeros((tm, LANES), F32)
    for k in range(TOP_K):
        rank = jnp.sum(jnp.where(hots[k], before, 0.0), axis=-1, keepdims=True)
        idx_out = jnp.where(lane == k, idxs[k], idx_out)
        wt_out = jnp.where(lane == k, exps[k] / denom, wt_out)
        rank_out = jnp.where(lane == k, rank, rank_out)
    idx_ref[...] = idx_out
    wt_ref[...] = wt_out
    rank_ref[...] = rank_out.astype(jnp.int32)
    cnt_ref[...] = carry_ref[...].astype(jnp.int32)


def _router(x, g, l, mod4, w_router, b_router, seq):
    m, d = x.shape
    tm = min(ROUTER_TM, m)
    per_b = seq // tm
    tok = lambda dt: jax.ShapeDtypeStruct((m, LANES), dt)
    tspec = pl.BlockSpec((tm, LANES), lambda i: (i, 0))
    return pl.pallas_call(
        _router_kernel,
        grid=(m // tm,),
        in_specs=[pl.BlockSpec((tm, d), lambda i: (i, 0)),
                  pl.BlockSpec((None, 1, d), lambda i: (l, 0, 0)),
                  pl.BlockSpec((None, None, 1, d), lambda i: (i // per_b, 4, 0, 0)),
                  pl.BlockSpec((None, None, 1, d), lambda i: (i // per_b, 3, 0, 0)),
                  pl.BlockSpec((None, d, LANES), lambda i: (l, 0, 0)),
                  pl.BlockSpec((None, 1, LANES), lambda i: (l, 0, 0))],
        out_specs=[pl.BlockSpec((tm, d), lambda i: (i, 0)), tspec, tspec, tspec,
                   pl.BlockSpec((1, LANES), lambda i: (0, 0))],
        out_shape=[jax.ShapeDtypeStruct((m, d), F32), tok(jnp.int32), tok(F32), tok(jnp.int32),
                   jax.ShapeDtypeStruct((1, LANES), jnp.int32)],
        scratch_shapes=[pltpu.VMEM((1, LANES), F32)],
        compiler_params=_params("arbitrary"),
        name="router",
    )(x, g, mod4, mod4, w_router, b_router)


def _row_copy(src_hbm, row, buf, r, sem):
    return pltpu.make_async_copy(src_hbm.at[pl.ds(row, 1), :], buf.at[pl.ds(r, 1), :], sem)


def _expert_kernel(be_ref, nu_ref, st_ref, h_hbm, wgu_ref, bgu_ref, wd_ref, bd_ref, o_ref,
                   xbuf0, xbuf1, sem):
    i = pl.program_id(0)
    n_used = nu_ref[0]

    def issue(blk, buf, s):
        base = blk * MOE_ROWS
        for r in range(MOE_ROWS):
            _row_copy(h_hbm, st_ref[base + r], buf, r, s).start()

    def drain(buf, s):
        for r in range(MOE_ROWS):
            _row_copy(h_hbm, 0, buf, r, s).wait()

    @pl.when(jnp.logical_and(i == 0, n_used > 0))
    def _():
        issue(0, xbuf0, sem.at[0])

    def step(cur, cur_sem, nxt, nxt_sem):
        drain(cur, cur_sem)
        issue(jnp.minimum(i + 1, n_used - 1), nxt, nxt_sem)
        gu = _dot(cur[...].astype(BF16), wgu_ref[...]) + bgu_ref[...]
        gate = jnp.minimum(gu[:, :D_EXPERT], SWIGLU_LIMIT)
        up = jnp.clip(gu[:, D_EXPERT:], -SWIGLU_LIMIT, SWIGLU_LIMIT)
        act = (up + 1.0) * gate * _sigmoid(SWIGLU_ALPHA * gate)
        o_ref[...] = _dot(act.astype(BF16), wd_ref[...]) + bd_ref[...]

        @pl.when(i == n_used - 1)
        def _():
            drain(nxt, nxt_sem)

    @pl.when(jnp.logical_and(i < n_used, i % 2 == 0))
    def _():
        step(xbuf0, sem.at[0], xbuf1, sem.at[1])

    @pl.when(jnp.logical_and(i < n_used, i % 2 == 1))
    def _():
        step(xbuf1, sem.at[1], xbuf0, sem.at[0])

    @pl.when(i >= n_used)
    def _():
        o_ref[...] = jnp.zeros_like(o_ref)


def _experts(h, block_e, n_used, slot_tok, w_gu, b_gu, w_down, b_down, l):
    n_slots = slot_tok.shape[0]
    d = h.shape[1]
    grid_spec = pltpu.PrefetchScalarGridSpec(
        num_scalar_prefetch=3,
        grid=(n_slots // MOE_ROWS,),
        in_specs=[pl.BlockSpec(memory_space=pl.ANY),
                  pl.BlockSpec((None, None, d, 2 * D_EXPERT), lambda i, be, nu, st: (l, be[i], 0, 0)),
                  pl.BlockSpec((None, None, 1, 2 * D_EXPERT), lambda i, be, nu, st: (l, be[i], 0, 0)),
                  pl.BlockSpec((None, None, D_EXPERT, d), lambda i, be, nu, st: (l, be[i], 0, 0)),
                  pl.BlockSpec((None, None, 1, d), lambda i, be, nu, st: (l, be[i], 0, 0))],
        out_specs=pl.BlockSpec((MOE_ROWS, d), lambda i, be, nu, st: (i, 0)),
        scratch_shapes=[pltpu.VMEM((MOE_ROWS, d), F32), pltpu.VMEM((MOE_ROWS, d), F32),
                        pltpu.SemaphoreType.DMA((2,))],
    )
    return pl.pallas_call(
        _expert_kernel,
        grid_spec=grid_spec,
        out_shape=jax.ShapeDtypeStruct((n_slots, d), F32),
        compiler_params=_params("arbitrary"),
        name="experts",
    )(block_e, n_used, slot_tok, h, w_gu, b_gu, w_down, b_down)


COMBINE_TM = 128


def _combine_kernel(dest_ref, ys_hbm, wt_ref, x_ref, gt_ref, o_ref, buf0, buf1, sem):
    i = pl.program_id(0)
    n = pl.num_programs(0)
    tm = COMBINE_TM

    def issue(tile, buf, s):
        base = tile * (tm * TOP_K)
        for r in range(tm):
            for k in range(TOP_K):
                _row_copy(ys_hbm, dest_ref[base + r * TOP_K + k], buf.at[k], r, s).start()

    def drain(buf, s):
        for r in range(tm):
            for k in range(TOP_K):
                _row_copy(ys_hbm, 0, buf.at[k], r, s).wait()

    @pl.when(i == 0)
    def _():
        issue(0, buf0, sem.at[0])

    def step(cur, cur_sem, nxt, nxt_sem):
        drain(cur, cur_sem)
        issue(jnp.minimum(i + 1, n - 1), nxt, nxt_sem)
        acc = None
        for k in range(TOP_K):
            term = wt_ref[:, k:k + 1] * cur[k]
            acc = term if acc is None else acc + term
        o_ref[...] = x_ref[...] + gt_ref[...] * acc

        @pl.when(i == n - 1)
        def _():
            drain(nxt, nxt_sem)

    @pl.when(i % 2 == 0)
    def _():
        step(buf0, sem.at[0], buf1, sem.at[1])

    @pl.when(i % 2 == 1)
    def _():
        step(buf1, sem.at[1], buf0, sem.at[0])


def _combine(dest, ys, wt, x, mod4, seq):
    m, d = x.shape
    tm = COMBINE_TM
    per_b = seq // tm
    grid_spec = pltpu.PrefetchScalarGridSpec(
        num_scalar_prefetch=1,
        grid=(m // tm,),
        in_specs=[pl.BlockSpec(memory_space=pl.ANY),
                  pl.BlockSpec((tm, LANES), lambda i, ds: (i, 0)),
                  pl.BlockSpec((tm, d), lambda i, ds: (i, 0)),
                  pl.BlockSpec((None, None, 1, d), lambda i, ds: (i // per_b, 5, 0, 0))],
        out_specs=pl.BlockSpec((tm, d), lambda i, ds: (i, 0)),
        scratch_shapes=[pltpu.VMEM((TOP_K, tm, d), F32), pltpu.VMEM((TOP_K, tm, d), F32),
                        pltpu.SemaphoreType.DMA((2,))],
    )
    return pl.pallas_call(
        _combine_kernel,
        grid_spec=grid_spec,
        out_shape=jax.ShapeDtypeStruct((m, d), F32),
        compiler_params=_params("arbitrary"),
        name="moe_combine",
    )(dest, ys, wt, x, mod4)


def _moe(x, g, l, mod4, w_router, b_router, w_gu, b_gu, w_down, b_down, seq):
    m = x.shape[0]
    h, idx, wt, rank, cnt = _router(x, g, l, mod4, w_router, b_router, seq)
    counts = cnt[0, :N_EXPERTS]
    padded = ((counts + MOE_ROWS - 1) // MOE_ROWS) * MOE_ROWS
    ends = jnp.cumsum(padded)
    starts = ends - padded
    dest = (starts[idx[:, :TOP_K]] + rank[:, :TOP_K]).reshape(-1)
    n_slots = m * TOP_K + N_EXPERTS * MOE_ROWS
    n_blocks = n_slots // MOE_ROWS
    slot_tok = jnp.zeros((n_slots,), jnp.int32).at[dest].set(jnp.arange(m * TOP_K, dtype=jnp.int32) // TOP_K)
    block_start = jnp.arange(n_blocks, dtype=jnp.int32) * MOE_ROWS
    block_e = jnp.minimum(jnp.sum(ends[None, :] <= block_start[:, None], axis=1), N_EXPERTS - 1).astype(jnp.int32)
    n_used = (ends[-1:] // MOE_ROWS).astype(jnp.int32)
    ys = _experts(h, block_e, n_used, slot_tok, w_gu, b_gu, w_down, b_down, l)
    return _combine(dest, ys, wt, x, mod4, seq)


def _pad_lanes(v):
    return jnp.pad(v, ((0, 0), (0, LANES - v.shape[-1])))[:, None, :]


def kernel(x, c, w_mod, b_mod, g_norm_mix, g_norm_ffn, w_in, hg_lower_bound, hg_onorm, s5_lambda_re, s5_lambda_im, s5_log_dt, s5_b_re, s5_b_im, s5_c_re, s5_c_im, s5_d, s5_w_glu, s5_b_glu, m2_conv_w, m2_conv_b, m2_dt_bias, m2_a_log, m2_d, m2_norm, rk_mu, rk_w0, rk_w2, rk_a0, rk_a2, rk_g2, rk_k_k, rk_k_a, rk_r_k, rk_ln_w, rk_ln_b, w_branch, w_out, w_router, b_router, w_gu, b_gu, w_down, b_down, g_final):
    batch, seq, d = x.shape
    depth = w_in.shape[0]
    n = batch * seq
    w_mod_b = w_mod.astype(BF16)
    w_in_b = w_in.astype(BF16)
    dt0 = N_MAIN
    w_tail = jnp.concatenate([w_in[:, :, dt0 + M2_HEADS:], w_in[:, :, dt0:dt0 + M2_HEADS],
                              jnp.zeros((depth, d, LANES - M2_HEADS), F32)], axis=-1).astype(BF16)
    w_branch_b = w_branch.astype(BF16)
    w_out_b = w_out.astype(BF16)
    w_gu_b = w_gu.astype(BF16)
    w_down_b = w_down.astype(BF16)
    w_router_p = jnp.pad(w_router, ((0, 0), (0, 0), (0, LANES - N_EXPERTS)))
    b_router_p = _pad_lanes(b_router)
    b_gu4 = b_gu[:, :, None, :]
    b_down4 = b_down[:, :, None, :]
    lbs = jax.nn.softmax(hg_lower_bound.astype(F32), axis=0)
    lbs = jnp.cumsum(lbs, axis=0) - lbs[0]
    dt_bias_p = _pad_lanes(m2_dt_bias)
    a_log_p = _pad_lanes(m2_a_log)
    m2_d_rep = jnp.repeat(m2_d, M2_HEADDIM, axis=-1)[:, None, :]
    c_pad = jnp.pad(c, ((0, 8 - batch), (0, 0)))

    x2 = x.reshape(n, d)
    for l in range(depth):
        mod = _modulation(c_pad, w_mod_b, b_mod[:, None, :], l)
        mod4 = mod[:batch].reshape(batch, 6, 1, d)
        h = _norm_modulate(x2, g_norm_mix[:, None, :], l, mod4, 0, 1, seq, BF16)
        proj = _matmul(h, w_in_b, l, N_MAIN, 512, 1024, F32, "in_proj_main")
        tail = _matmul(h, w_tail, l, N_TAIL, 512, 640, F32, "in_proj_tail")
        ya = _hgrn2(proj, lbs[l][None], hg_onorm[l][None], batch, seq)
        tables = _s5_tables(s5_lambda_re[l], s5_lambda_im[l], s5_log_dt[l], s5_b_re[l], s5_b_im[l],
                            s5_c_re[l], s5_c_im[l], seq // S5_T)
        yb = _s5(proj, tables, s5_d[l][None], s5_w_glu[l].astype(BF16), s5_b_glu[l][None], batch, seq)
        yc = _mamba2(proj, tail, m2_conv_w[l], m2_conv_b[l][None], dt_bias_p[l], a_log_p[l], m2_d_rep[l],
                     m2_norm[l][None], batch, seq)
        yd = _rwkv7(tail, rk_mu[l][None], rk_w0[l][None], rk_w2[l], rk_a0[l][None], rk_a2[l], rk_g2[l],
                    rk_k_k[l][None], rk_k_a[l][None], rk_r_k[l].reshape(1, BRANCH), rk_ln_w[l][None],
                    rk_ln_b[l][None], batch, seq)
        merged = _merge([ya, yb, yc, yd], proj, w_branch_b, l)
        x2 = _out_proj_residual(merged, w_out_b, l, x2, mod4, 2, seq)
        x2 = _moe(x2, g_norm_ffn[:, None, :], l, mod4, w_router_p, b_router_p, w_gu_b, b_gu4,
                  w_down_b, b_down4, seq)
    return _final_norm(x2, g_final[None]).reshape(batch, seq, d)
```

```python
import functools
import math

import jax
import jax.numpy as jnp
from jax import lax
from jax.experimental import pallas as pl
from jax.experimental.pallas import tpu as pltpu

F32 = jnp.float32
BF16 = jnp.bfloat16
HI = lax.Precision.HIGHEST

D_MODEL = 2048
CHUNK = 64
BRANCH = 512
HG_HEADS, HG_DK = 4, 128
S5_CH, S5_GROUPS, S5_STATE = 16, 32, 64
S5_T = 16
M2_HEADS, M2_HEADDIM, M2_GROUPS, M2_STATE, M2_CONV, M2_XBC = 8, 64, 2, 128, 4, 1024
RK_HEADS, RK_HEADSIZE = 8, 64
RK_DECAY_LORA, RK_A_LORA, RK_GATE_LORA, RK_WIDTH = 64, 64, 128, 1792
RK_GN_EPS = 64e-5
GATE_IN = 4 * D_MODEL
N_MAIN = GATE_IN + 4 * BRANCH + BRANCH + BRANCH + M2_XBC
N_TAIL = RK_WIDTH + 128
N_EXPERTS, TOP_K, D_EXPERT = 32, 4, 768
SWIGLU_LIMIT, SWIGLU_ALPHA = 7.0, 1.702
MOE_ROWS = 256
NORM_EPS = 1e-6
LANES = 128
VMEM_LIMIT = 48 * 1024 * 1024


def _dot(a, b, prec=None):
    return jnp.dot(a, b, preferred_element_type=F32, precision=prec)


def _dot_nt(a, b, prec=None):
    return lax.dot_general(a, b, (((1,), (1,)), ((), ())), preferred_element_type=F32, precision=prec)


def _dot_tn(a, b, prec=None):
    return lax.dot_general(a, b, (((0,), (0,)), ((), ())), preferred_element_type=F32, precision=prec)


def _bdot(a, b):
    return _dot(a.astype(BF16), b.astype(BF16))


def _bdot_nt(a, b):
    return _dot_nt(a.astype(BF16), b.astype(BF16))


def _bdot_tn(a, b):
    return _dot_tn(a.astype(BF16), b.astype(BF16))


def _sigmoid(x):
    return 1.0 / (1.0 + jnp.exp(-x))


def _silu(x):
    return x * _sigmoid(x)


def _softplus(x):
    return jnp.maximum(x, 0.0) + jnp.log(1.0 + jnp.exp(-jnp.abs(x)))


def _tril(n, k=0):
    r = lax.broadcasted_iota(jnp.int32, (n, n), 0)
    c = lax.broadcasted_iota(jnp.int32, (n, n), 1)
    return (c - r) <= k


def _params(*sem):
    return pltpu.CompilerParams(dimension_semantics=sem, vmem_limit_bytes=VMEM_LIMIT)


def _mm_kernel(x_ref, w_ref, o_ref, wbf_ref):
    @pl.when(pl.program_id(1) == 0)
    def _():
        wbf_ref[...] = w_ref[...].astype(BF16)

    o_ref[...] = _dot(x_ref[...], wbf_ref[...]).astype(o_ref.dtype)


def _mod_kernel(c_ref, w_ref, b_ref, o_ref):
    o_ref[...] = _dot(_silu(c_ref[...]).astype(BF16), w_ref[...].astype(BF16)) + b_ref[...]


def _mm_resid_kernel(x_ref, w_ref, r_ref, g_ref, o_ref):
    o_ref[...] = r_ref[...] + g_ref[...] * _dot(x_ref[...], w_ref[...])


def _matmul(x, w, l, n_cols, tm, tn, out_dtype, name):
    m, k = x.shape
    return pl.pallas_call(
        _mm_kernel,
        grid=(n_cols // tn, m // tm),
        in_specs=[pl.BlockSpec((tm, k), lambda j, i: (i, 0)),
                  pl.BlockSpec((None, k, tn), lambda j, i: (l, 0, j))],
        out_specs=pl.BlockSpec((tm, tn), lambda j, i: (i, j)),
        out_shape=jax.ShapeDtypeStruct((m, n_cols), out_dtype),
        scratch_shapes=[pltpu.VMEM((k, tn), BF16)],
        compiler_params=_params("parallel", "arbitrary"),
        name=name,
    )(x, w)


def _modulation(cond, w_mod, b_mod, l):
    m, k = cond.shape
    n = w_mod.shape[-1]
    tn = 1024
    return pl.pallas_call(
        _mod_kernel,
        grid=(n // tn,),
        in_specs=[pl.BlockSpec((m, k), lambda j: (0, 0)),
                  pl.BlockSpec((None, k, tn), lambda j: (l, 0, j)),
                  pl.BlockSpec((None, 1, tn), lambda j: (l, 0, j))],
        out_specs=pl.BlockSpec((m, tn), lambda j: (0, j)),
        out_shape=jax.ShapeDtypeStruct((m, n), F32),
        compiler_params=_params("parallel"),
        name="modulation",
    )(cond, w_mod, b_mod)


def _out_proj_residual(merged, w_out, l, x, mod4, gate_idx, seq):
    m, k = merged.shape
    tm, tn = 512, 512
    per_b = seq // tm
    return pl.pallas_call(
        _mm_resid_kernel,
        grid=(D_MODEL // tn, m // tm),
        in_specs=[pl.BlockSpec((tm, k), lambda j, i: (i, 0)),
                  pl.BlockSpec((None, k, tn), lambda j, i: (l, 0, j)),
                  pl.BlockSpec((tm, tn), lambda j, i: (i, j)),
                  pl.BlockSpec((None, None, 1, tn), lambda j, i: (i // per_b, gate_idx, 0, j))],
        out_specs=pl.BlockSpec((tm, tn), lambda j, i: (i, j)),
        out_shape=jax.ShapeDtypeStruct((m, D_MODEL), F32),
        compiler_params=_params("parallel", "parallel"),
        name="out_proj_residual",
    )(merged, w_out, x, mod4)


def _rms(x, g):
    return x * lax.rsqrt(jnp.mean(x * x, axis=-1, keepdims=True) + NORM_EPS) * g


def _normmod_kernel(x_ref, g_ref, sc_ref, sh_ref, o_ref):
    y = _rms(x_ref[...], g_ref[...])
    o_ref[...] = (y * (1.0 + sc_ref[...]) + sh_ref[...]).astype(o_ref.dtype)


def _norm_modulate(x, g, l, mod4, shift_idx, scale_idx, seq, out_dtype):
    m, d = x.shape
    ts = 512
    per_b = seq // ts
    return pl.pallas_call(
        _normmod_kernel,
        grid=(m // ts,),
        in_specs=[pl.BlockSpec((ts, d), lambda i: (i, 0)),
                  pl.BlockSpec((None, 1, d), lambda i: (l, 0, 0)),
                  pl.BlockSpec((None, None, 1, d), lambda i: (i // per_b, scale_idx, 0, 0)),
                  pl.BlockSpec((None, None, 1, d), lambda i: (i // per_b, shift_idx, 0, 0))],
        out_specs=pl.BlockSpec((ts, d), lambda i: (i, 0)),
        out_shape=jax.ShapeDtypeStruct((m, d), out_dtype),
        compiler_params=_params("parallel"),
        name="norm_modulate",
    )(x, g, mod4, mod4)


def _final_norm_kernel(x_ref, g_ref, o_ref):
    o_ref[...] = _rms(x_ref[...], g_ref[...])


def _final_norm(x, g):
    m, d = x.shape
    ts = 512
    return pl.pallas_call(
        _final_norm_kernel,
        grid=(m // ts,),
        in_specs=[pl.BlockSpec((ts, d), lambda i: (i, 0)), pl.BlockSpec((1, d), lambda i: (0, 0))],
        out_specs=pl.BlockSpec((ts, d), lambda i: (i, 0)),
        out_shape=jax.ShapeDtypeStruct((m, d), F32),
        compiler_params=_params("parallel"),
        name="final_norm",
    )(x, g)


HG_SUB = 16


def _hgrn2_kernel(q_ref, f_ref, i_ref, g_ref, lb_ref, on_ref, o_ref, state_ref):
    @pl.when(pl.program_id(1) == 0)
    def _():
        state_ref[...] = jnp.zeros_like(state_ref)

    sub_mask = _tril(HG_SUB)
    lb = lb_ref[...]
    forget = lb + (1.0 - lb) * _sigmoid(f_ref[...])
    k = 1.0 - forget
    q = _silu(q_ref[...])
    v = i_ref[...]
    cum = _dot(_tril(CHUNK).astype(F32), jnp.log(forget), HI)
    last = cum[CHUNK - 1:CHUNK, :]
    q_dec = q * jnp.exp(cum)
    k_end = k * jnp.exp(last - cum)
    heads = range(HG_HEADS)
    sls = [slice(h * HG_DK, (h + 1) * HG_DK) for h in heads]
    sts = [state_ref[h] for h in heads]
    o_inter = [_bdot_nt(q_dec[:, sls[h]], sts[h]) for h in heads]
    for h in heads:
        state_ref[h] = sts[h] * jnp.exp(last[:, sls[h]]) + _bdot_tn(v[:, sls[h]], k_end[:, sls[h]])
    rows = [[None] * (CHUNK // HG_SUB) for _ in heads]
    for i in range(CHUNK // HG_SUB):
        r0 = i * HG_SUB
        if i > 0:
            ci = cum[r0 - 1:r0, :]
            qp = q[r0:r0 + HG_SUB] * jnp.exp(cum[r0:r0 + HG_SUB] - ci)
            kp = k[:r0] * jnp.exp(ci - cum[:r0])
            sc = [_bdot_nt(qp[:, sls[h]], kp[:, sls[h]]) for h in heads]
            off = [_bdot(sc[h], v[:r0, sls[h]]) for h in heads]
        for h in heads:
            sl = sls[h]
            qi = q[r0:r0 + HG_SUB, sl]
            cumi = cum[r0:r0 + HG_SUB, sl]
            oi = o_inter[h][r0:r0 + HG_SUB]
            if i > 0:
                oi = oi + off[h]
            for s in range(HG_SUB):
                diff = jnp.where(sub_mask[:, s:s + 1], cumi - cumi[s:s + 1, :], -jnp.inf)
                w = jnp.sum(qi * jnp.exp(diff) * k[r0 + s:r0 + s + 1, sl], axis=-1, keepdims=True)
                oi = oi + w * v[r0 + s:r0 + s + 1, sl]
            rows[h][i] = oi
    outs = [_rms(jnp.concatenate(rows[h], axis=0), on_ref[...]) for h in heads]
    o_ref[...] = jnp.concatenate(outs, axis=-1) * _silu(g_ref[...])


def _hgrn2(proj, lb, onorm, batch, seq):
    nc = seq // CHUNK
    col0 = GATE_IN // BRANCH
    spec = lambda off: pl.BlockSpec((CHUNK, BRANCH), lambda b, c: (b * nc + c, col0 + off))
    vec = lambda n: pl.BlockSpec((1, n), lambda b, c: (0, 0))
    return pl.pallas_call(
        _hgrn2_kernel,
        grid=(batch, nc),
        in_specs=[spec(0), spec(1), spec(2), spec(3), vec(BRANCH), vec(HG_DK)],
        out_specs=pl.BlockSpec((CHUNK, BRANCH), lambda b, c: (b * nc + c, 0)),
        out_shape=jax.ShapeDtypeStruct((batch * seq, BRANCH), F32),
        scratch_shapes=[pltpu.VMEM((HG_HEADS, HG_DK, HG_DK), F32)],
        compiler_params=_params("parallel", "arbitrary"),
        name="hgrn2",
    )(proj, proj, proj, proj, lb, onorm)


def _mamba2_kernel(z_ref, xbc_ref, dt_ref, cw_ref, cb_ref, dtb_ref, alog_ref, d_ref, ng_ref,
                   o_ref, ext_ref, state_ref):
    @pl.when(pl.program_id(1) == 0)
    def _():
        ext_ref[0:8, :] = jnp.zeros((8, M2_XBC), F32)
        state_ref[...] = jnp.zeros_like(state_ref)

    ext_ref[8:8 + CHUNK, :] = xbc_ref[...]
    conv = cb_ref[...]
    for j in range(M2_CONV):
        conv = conv + cw_ref[j:j + 1, :] * ext_ref[8 - (M2_CONV - 1) + j:8 - (M2_CONV - 1) + j + CHUNK, :]
    ext_ref[0:8, :] = xbc_ref[CHUNK - 8:CHUNK, :]
    xa = _silu(conv)
    gn = M2_GROUPS * M2_STATE
    xs = xa[:, :BRANCH]
    dt = _softplus(dt_ref[...] + dtb_ref[...])
    adt = -jnp.exp(alog_ref[...]) * dt
    tri = _tril(CHUNK)
    cum = _dot(tri.astype(F32), adt, HI)
    cum_t = cum.T
    cb = []
    for g in range(M2_GROUPS):
        bm = xa[:, BRANCH + g * M2_STATE:BRANCH + (g + 1) * M2_STATE]
        cm = xa[:, BRANCH + gn + g * M2_STATE:BRANCH + gn + (g + 1) * M2_STATE]
        cb.append((bm, cm, _bdot_nt(cm, bm)))
    ys = []
    for h in range(M2_HEADS):
        bm, cm, cbg = cb[h // (M2_HEADS // M2_GROUPS)]
        cum_h = cum[:, h:h + 1]
        lmat = jnp.exp(jnp.where(tri, cum_h - cum_t[h:h + 1, :], -jnp.inf))
        x_h = xs[:, h * M2_HEADDIM:(h + 1) * M2_HEADDIM]
        xdt = x_h * dt[:, h:h + 1]
        st = state_ref[h]
        y = _bdot(cbg * lmat, xdt) + jnp.exp(cum_h) * _bdot(cm, st)
        last = cum[CHUNK - 1:CHUNK, h:h + 1]
        state_ref[h] = jnp.exp(last) * st + _bdot_tn(bm * jnp.exp(last - cum_h), xdt)
        ys.append(y)
    y = jnp.concatenate(ys, axis=-1) + d_ref[...] * xs
    y = y * _silu(z_ref[...])
    gw = BRANCH // M2_GROUPS
    outs = [_rms(y[:, g * gw:(g + 1) * gw], ng_ref[:, g * gw:(g + 1) * gw]) for g in range(M2_GROUPS)]
    o_ref[...] = jnp.concatenate(outs, axis=-1)


def _mamba2(proj, tail, conv_w, conv_b, dt_bias, a_log, d_rep, norm_g, batch, seq):
    nc = seq // CHUNK
    zcol = (GATE_IN + 5 * BRANCH) // BRANCH
    xcol = (GATE_IN + 6 * BRANCH) // M2_XBC
    vec = lambda r, n: pl.BlockSpec((r, n), lambda b, c: (0, 0))
    return pl.pallas_call(
        _mamba2_kernel,
        grid=(batch, nc),
        in_specs=[pl.BlockSpec((CHUNK, BRANCH), lambda b, c: (b * nc + c, zcol)),
                  pl.BlockSpec((CHUNK, M2_XBC), lambda b, c: (b * nc + c, xcol)),
                  pl.BlockSpec((CHUNK, LANES), lambda b, c: (b * nc + c, RK_WIDTH // LANES)),
                  vec(M2_CONV, M2_XBC), vec(1, M2_XBC), vec(1, LANES), vec(1, LANES),
                  vec(1, BRANCH), vec(1, BRANCH)],
        out_specs=pl.BlockSpec((CHUNK, BRANCH), lambda b, c: (b * nc + c, 0)),
        out_shape=jax.ShapeDtypeStruct((batch * seq, BRANCH), F32),
        scratch_shapes=[pltpu.VMEM((8 + CHUNK, M2_XBC), F32),
                        pltpu.VMEM((M2_HEADS, M2_STATE, M2_HEADDIM), F32)],
        compiler_params=_params("parallel", "arbitrary"),
        name="mamba2",
    )(proj, proj, tail, conv_w, conv_b, dt_bias, a_log, d_rep, norm_g)


def _unit_lower_inverses(mats):
    n = mats[0].shape[0]
    eye = (lax.broadcasted_iota(jnp.int32, (n, n), 0) == lax.broadcasted_iota(jnp.int32, (n, n), 1)).astype(F32)
    invs = [eye - a for a in mats]
    pows = list(mats)
    k = 2
    while k < n:
        pows = [_bdot(p, p) for p in pows]
        invs = [_bdot(inv, eye + p) for inv, p in zip(invs, pows)]
        k *= 2
    return invs


def _rwkv7_kernel(p_ref, mu_ref, w0_ref, w2_ref, a0_ref, a2_ref, g2_ref, kk_ref, ka_ref, rk_ref,
                  lnw_ref, lnb_ref, o_ref, ext_ref, state_ref):
    @pl.when(pl.program_id(1) == 0)
    def _():
        ext_ref[0:8, :] = jnp.zeros((8, RK_WIDTH), F32)
        state_ref[...] = jnp.zeros_like(state_ref)

    p = p_ref[...]
    ext_ref[8:8 + CHUNK, :] = p
    prev = ext_ref[7:7 + CHUNK, :]
    ext_ref[0:8, :] = p_ref[CHUNK - 8:CHUNK, :]
    pm = p + (prev - p) * mu_ref[...]
    b3 = 3 * BRANCH
    r = pm[:, :BRANCH]
    k = pm[:, BRANCH:2 * BRANCH]
    v = pm[:, 2 * BRANCH:b3]
    w_lo = pm[:, b3:b3 + RK_DECAY_LORA]
    a_lo = pm[:, b3 + RK_DECAY_LORA:b3 + RK_DECAY_LORA + RK_A_LORA]
    g_lo = pm[:, b3 + RK_DECAY_LORA + RK_A_LORA:]
    w = -_softplus(-(w0_ref[...] + _bdot(jnp.tanh(w_lo), w2_ref[...]))) - 0.5
    logw = -jnp.exp(w)
    a = _sigmoid(a0_ref[...] + _bdot(a_lo, a2_ref[...]))
    g = _bdot(_sigmoid(g_lo), g2_ref[...])
    kk = k * kk_ref[...]
    k = k * (1.0 + (a - 1.0) * ka_ref[...])
    c, hs = CHUNK, RK_HEADSIZE
    row = lax.broadcasted_iota(jnp.int32, (2 * c, 2 * c), 0)
    col = lax.broadcasted_iota(jnp.int32, (2 * c, 2 * c), 1)
    quad_mask = (col & (c - 1)) < (row & (c - 1)) + (row >= c).astype(jnp.int32)
    cum = _dot(_tril(c).astype(F32), logw, HI)
    last = cum[c - 1:c, :]
    gam = jnp.exp(cum)
    ginv = jnp.exp(-cum)
    to_end = jnp.exp(last - cum)
    gam_prev = jnp.exp(cum - logw)
    heads = range(RK_HEADS)
    sls = [slice(h * hs, (h + 1) * hs) for h in heads]
    kap = []
    for sl in sls:
        kh = kk[:, sl]
        kap.append(kh / jnp.maximum(jnp.sqrt(jnp.sum(kh * kh, axis=-1, keepdims=True)), 1e-12))
    beta = [kap[h] * a[:, sls[h]] for h in heads]
    lhs = [jnp.concatenate([kap[h] * gam_prev[:, sls[h]], r[:, sls[h]] * gam[:, sls[h]]], axis=0)
           for h in heads]
    rhs = [jnp.concatenate([beta[h] * ginv[:, sls[h]], k[:, sls[h]] * ginv[:, sls[h]]], axis=0)
           for h in heads]
    quad = [jnp.where(quad_mask, _bdot_nt(lhs[h], rhs[h]), 0.0) for h in heads]
    invs = _unit_lower_inverses([q[:c, :c] for q in quad])
    vs = [v[:, sl] for sl in sls]
    akv = [_bdot(quad[h][:, c:], vs[h]) for h in heads]
    sts = [state_ref[h] for h in heads]
    from_state = [_bdot_nt(lhs[h], sts[h]) for h in heads]
    us = [_bdot(invs[h], -(from_state[h][:c] + akv[h][:c])) for h in heads]
    ys = [from_state[h][c:] + akv[h][c:] + _bdot(quad[h][c:, :c], us[h]) for h in heads]
    for h in heads:
        sl = sls[h]
        upd = _bdot_tn(jnp.concatenate([us[h], vs[h]], axis=0),
                       jnp.concatenate([beta[h] * to_end[:, sl], k[:, sl] * to_end[:, sl]], axis=0))
        state_ref[h] = sts[h] * gam[c - 1:c, sl] + upd
    outs = []
    for h in heads:
        sl = sls[h]
        mean = jnp.mean(ys[h], axis=-1, keepdims=True)
        yc = ys[h] - mean
        var = jnp.mean(yc * yc, axis=-1, keepdims=True)
        yn = yc * lax.rsqrt(var + RK_GN_EPS) * lnw_ref[:, sl] + lnb_ref[:, sl]
        bonus = jnp.sum(r[:, sl] * k[:, sl] * rk_ref[:, sl], axis=-1, keepdims=True) * vs[h]
        outs.append(yn + bonus)
    o_ref[...] = jnp.concatenate(outs, axis=-1) * g


def _rwkv7(tail, mu, w0, w2, a0, a2, g2, k_k, k_a, r_k, ln_w, ln_b, batch, seq):
    nc = seq // CHUNK
    vec = lambda r, n: pl.BlockSpec((r, n), lambda b, c: (0, 0))
    return pl.pallas_call(
        _rwkv7_kernel,
        grid=(batch, nc),
        in_specs=[pl.BlockSpec((CHUNK, RK_WIDTH), lambda b, c: (b * nc + c, 0)),
                  vec(1, RK_WIDTH), vec(1, BRANCH), vec(RK_DECAY_LORA, BRANCH), vec(1, BRANCH),
                  vec(RK_A_LORA, BRANCH), vec(RK_GATE_LORA, BRANCH), vec(1, BRANCH), vec(1, BRANCH),
                  vec(1, BRANCH), vec(1, BRANCH), vec(1, BRANCH)],
        out_specs=pl.BlockSpec((CHUNK, BRANCH), lambda b, c: (b * nc + c, 0)),
        out_shape=jax.ShapeDtypeStruct((batch * seq, BRANCH), F32),
        scratch_shapes=[pltpu.VMEM((8 + CHUNK, RK_WIDTH), F32),
                        pltpu.VMEM((RK_HEADS, RK_HEADSIZE, RK_HEADSIZE), F32)],
        compiler_params=_params("parallel", "arbitrary"),
        name="rwkv7",
    )(tail, mu, w0, w2, a0, a2, g2, k_k, k_a, r_k, ln_w, ln_b)


def _s5_tables(lam_re, lam_im, log_dt, b_re, b_im, c_re, c_im, n_chunks):
    t = S5_T
    g, p, i = b_re.shape
    dt = jnp.exp(log_dt)[:, None]

    def apow(n):
        e = n.astype(F32)[..., None, None]
        mag = jnp.exp(e * lam_re * dt)
        return mag * jnp.cos(e * lam_im * dt), mag * jnp.sin(e * lam_im * dt)

    ab_re, ab_im = apow(jnp.ones((), F32))
    den = lam_re * lam_re + lam_im * lam_im
    nr = ab_re - 1.0
    coef_re = ((nr * lam_re + ab_im * lam_im) / den)[..., None]
    coef_im = ((ab_im * lam_re - nr * lam_im) / den)[..., None]
    bt_re = coef_re * b_re - coef_im * b_im
    bt_im = coef_re * b_im + coef_im * b_re
    ar, ai = apow(jnp.arange(t + 1))
    cr, ci = c_re[None], c_im[None]
    ca_re = cr * ar[:, :, None, :] - ci * ai[:, :, None, :]
    ca_im = cr * ai[:, :, None, :] + ci * ar[:, :, None, :]
    kern = (jnp.einsum('tgop,gpi->tgoi', ca_re[:t], bt_re, precision=HI)
            - jnp.einsum('tgop,gpi->tgoi', ca_im[:t], bt_im, precision=HI))
    lag = jnp.arange(t)[None, :] - jnp.arange(t)[:, None]
    toep = jnp.where((lag >= 0)[:, :, None, None, None], kern[jnp.maximum(lag, 0)], 0.0)
    toep = toep.transpose(2, 0, 4, 1, 3).reshape(g, t * i, t * i)
    br, bi = ar[:t][::-1], ai[:t][::-1]
    bs_re = br[..., None] * bt_re - bi[..., None] * bt_im
    bs_im = br[..., None] * bt_im + bi[..., None] * bt_re
    bst = jnp.concatenate([bs_re, bs_im], axis=2).transpose(1, 0, 3, 2).reshape(g, t * i, 2 * p)
    cst = jnp.concatenate([ca_re[1:], -ca_im[1:]], axis=3)
    cst = cst.transpose(1, 3, 0, 2).reshape(g, 2 * p, t * i)
    levels = max(1, int(math.log2(n_chunks)))
    pr, pi = apow(t * (2 ** jnp.arange(levels)))
    m1 = jnp.concatenate([pr, pr], axis=-1)[:, :, None, :]
    m2 = jnp.concatenate([-pi, pi], axis=-1)[:, :, None, :]
    return toep.astype(BF16), bst.astype(BF16), cst.astype(BF16), m1, m2


def _s5_kernel(n_chunks, u_ref, toep_ref, bst_ref, cst_ref, m1_ref, m2_ref, y_ref):
    u = u_ref[...]
    y = _dot(u, toep_ref[...])
    x = _dot(u, bst_ref[...])
    rows = lax.broadcasted_iota(jnp.int32, x.shape, 0) & (n_chunks - 1)
    half = x.shape[1] // 2

    def shifted(val, d):
        return jnp.where(rows >= d, pltpu.roll(val, d, axis=0), 0.0)

    d = 1
    lvl = 0
    while d < n_chunks:
        xs = shifted(x, d)
        x = x + xs * m1_ref[lvl] + pltpu.roll(xs, half, axis=1) * m2_ref[lvl]
        d *= 2
        lvl += 1
    y_ref[...] = y + _dot(shifted(x, 1).astype(BF16), cst_ref[...])


def _s5_scan(u_g, toep, bst, cst, m1, m2, n_chunks):
    g, r, w = u_g.shape
    p2 = bst.shape[-1]
    levels = m1.shape[0]
    return pl.pallas_call(
        functools.partial(_s5_kernel, n_chunks),
        grid=(g,),
        in_specs=[pl.BlockSpec((None, r, w), lambda i: (i, 0, 0)),
                  pl.BlockSpec((None, w, w), lambda i: (i, 0, 0)),
                  pl.BlockSpec((None, w, p2), lambda i: (i, 0, 0)),
                  pl.BlockSpec((None, p2, w), lambda i: (i, 0, 0)),
                  pl.BlockSpec((levels, None, 1, p2), lambda i: (0, i, 0, 0)),
                  pl.BlockSpec((levels, None, 1, p2), lambda i: (0, i, 0, 0))],
        out_specs=pl.BlockSpec((None, r, w), lambda i: (i, 0, 0)),
        out_shape=jax.ShapeDtypeStruct((g, r, w), F32),
        compiler_params=_params("parallel"),
        name="s5_scan",
    )(u_g, toep, bst, cst, m1, m2)


def _s5_post_kernel(y_ref, u_ref, d_ref, w_ref, b_ref, o_ref):
    y = y_ref[...] + d_ref[...] * u_ref[...]
    y = 0.5 * y * (1.0 + jnp.tanh(math.sqrt(2.0 / math.pi) * (y + 0.044715 * (y * y * y))))
    o_ref[...] = y * _sigmoid(_bdot(y, w_ref[...]) + b_ref[...])


def _s5_post(y_ssm, proj, d_skip, w_glu, b_glu):
    m = y_ssm.shape[0]
    tm = min(512, m)
    ucol = (GATE_IN + 4 * BRANCH) // BRANCH
    return pl.pallas_call(
        _s5_post_kernel,
        grid=(m // tm,),
        in_specs=[pl.BlockSpec((tm, BRANCH), lambda i: (i, 0)),
                  pl.BlockSpec((tm, BRANCH), lambda i: (i, ucol)),
                  pl.BlockSpec((1, BRANCH), lambda i: (0, 0)),
                  pl.BlockSpec((BRANCH, BRANCH), lambda i: (0, 0)),
                  pl.BlockSpec((1, BRANCH), lambda i: (0, 0))],
        out_specs=pl.BlockSpec((tm, BRANCH), lambda i: (i, 0)),
        out_shape=jax.ShapeDtypeStruct((m, BRANCH), F32),
        compiler_params=_params("parallel"),
        name="s5_post",
    )(y_ssm, proj, d_skip, w_glu, b_glu)


def _s5(proj, tables, d_skip, w_glu, b_glu, batch, seq):
    n = batch * seq
    nc = seq // S5_T
    u0 = GATE_IN + 4 * BRANCH
    u = proj[:, u0:u0 + BRANCH].reshape(batch, nc, S5_T, S5_GROUPS, S5_CH)
    u_g = u.transpose(3, 0, 1, 2, 4).reshape(S5_GROUPS, batch * nc, S5_T * S5_CH).astype(BF16)
    y_g = _s5_scan(u_g, *tables, nc)
    y = y_g.reshape(S5_GROUPS, batch, nc, S5_T, S5_CH).transpose(1, 2, 3, 0, 4).reshape(n, BRANCH)
    return _s5_post(y, proj, d_skip, w_glu, b_glu)


def _merge_kernel(ya_ref, yb_ref, yc_ref, yd_ref, ga_ref, gb_ref, gc_ref, gd_ref,
                  wa_ref, wb_ref, wc_ref, wd_ref, o_ref):
    acc = None
    for y_ref, g_ref, w_ref in ((ya_ref, ga_ref, wa_ref), (yb_ref, gb_ref, wb_ref),
                                (yc_ref, gc_ref, wc_ref), (yd_ref, gd_ref, wd_ref)):
        term = _sigmoid(g_ref[...]) * _dot(y_ref[...].astype(BF16), w_ref[...])
        acc = term if acc is None else acc + term
    o_ref[...] = acc.astype(o_ref.dtype)


def _merge(ys, proj, w_branch, l):
    m = ys[0].shape[0]
    tm = 256
    yspec = pl.BlockSpec((tm, BRANCH), lambda i: (i, 0))
    gspec = lambda n: pl.BlockSpec((tm, D_MODEL), lambda i: (i, n))
    wspec = lambda n: pl.BlockSpec((None, None, BRANCH, D_MODEL), lambda i: (l, n, 0, 0))
    return pl.pallas_call(
        _merge_kernel,
        grid=(m // tm,),
        in_specs=[yspec] * 4 + [gspec(n) for n in range(4)] + [wspec(n) for n in range(4)],
        out_specs=pl.BlockSpec((tm, D_MODEL), lambda i: (i, 0)),
        out_shape=jax.ShapeDtypeStruct((m, D_MODEL), BF16),
        compiler_params=_params("parallel"),
        name="branch_merge",
    )(*ys, proj, proj, proj, proj, w_branch, w_branch, w_branch, w_branch)


ROUTER_TM = 256


def _router_kernel(x_ref, g_ref, sc_ref, sh_ref, wr_ref, br_ref,
                   h_ref, idx_ref, wt_ref, rank_ref, cnt_ref, carry_ref):
    @pl.when(pl.program_id(0) == 0)
    def _():
        carry_ref[...] = jnp.zeros_like(carry_ref)

    h = _rms(x_ref[...], g_ref[...]) * (1.0 + sc_ref[...]) + sh_ref[...]
    h_ref[...] = h
    tm = h.shape[0]
    lane = lax.broadcasted_iota(jnp.int32, (tm, LANES), 1)
    logits = _dot(h, wr_ref[...], HI) + br_ref[...]
    masked = jnp.where(lane < N_EXPERTS, logits, -jnp.inf)
    vals, hots, idxs = [], [], []
    for _ in range(TOP_K):
        m = jnp.max(masked, axis=-1, keepdims=True)
        idx = jnp.min(jnp.where(masked == m, lane, LANES), axis=-1, keepdims=True)
        hot = lane == idx
        masked = jnp.where(hot, -jnp.inf, masked)
        vals.append(m)
        hots.append(hot)
        idxs.append(idx)
    exps = [jnp.exp(v - vals[0]) for v in vals]
    denom = exps[0] + exps[1] + exps[2] + exps[3]
    multi = sum(hot.astype(F32) for hot in hots)
    before = _dot(_tril(tm, -1).astype(BF16), multi.astype(BF16)) + carry_ref[...]
    carry_ref[...] = carry_ref[...] + jnp.sum(multi, axis=0, keepdims=True)
    idx_out = jnp.zeros((tm, LANES), jnp.int32)
    wt_out = jnp.zeros((tm, LANES), F32)
    rank_out = jnp.zeros((tm, LANES), F32)
    for k in range(TOP_K):
        rank = jnp.sum(jnp.where(hots[k], before, 0.0), axis=-1, keepdims=True)
        idx_out = jnp.where(lane == k, idxs[k], idx_out)
        wt_out = jnp.where(lane == k, exps[k] / denom, wt_out)
        rank_out = jnp.where(lane == k, rank, rank_out)
    idx_ref[...] = idx_out
    wt_ref[...] = wt_out
    rank_ref[...] = rank_out.astype(jnp.int32)
    cnt_ref[...] = carry_ref[...].astype(jnp.int32)


def _router(x, g, l, mod4, w_router, b_router, seq):
    m, d = x.shape
    tm = min(ROUTER_TM, m)
    per_b = seq // tm
    tok = lambda dt: jax.ShapeDtypeStruct((m, LANES), dt)
    tspec = pl.BlockSpec((tm, LANES), lambda i: (i, 0))
    return pl.pallas_call(
        _router_kernel,
        grid=(m // tm,),
        in_specs=[pl.BlockSpec((tm, d), lambda i: (i, 0)),
                  pl.BlockSpec((None, 1, d), lambda i: (l, 0, 0)),
                  pl.BlockSpec((None, None, 1, d), lambda i: (i // per_b, 4, 0, 0)),
                  pl.BlockSpec((None, None, 1, d), lambda i: (i // per_b, 3, 0, 0)),
                  pl.BlockSpec((None, d, LANES), lambda i: (l, 0, 0)),
                  pl.BlockSpec((None, 1, LANES), lambda i: (l, 0, 0))],
        out_specs=[pl.BlockSpec((tm, d), lambda i: (i, 0)), tspec, tspec, tspec,
                   pl.BlockSpec((1, LANES), lambda i: (0, 0))],
        out_shape=[jax.ShapeDtypeStruct((m, d), F32), tok(jnp.int32), tok(F32), tok(jnp.int32),
                   jax.ShapeDtypeStruct((1, LANES), jnp.int32)],
        scratch_shapes=[pltpu.VMEM((1, LANES), F32)],
        compiler_params=_params("arbitrary"),
        name="router",
    )(x, g, mod4, mod4, w_router, b_router)


def _row_copy(src_hbm, row, buf, r, sem):
    return pltpu.make_async_copy(src_hbm.at[pl.ds(row, 1), :], buf.at[pl.ds(r, 1), :], sem)


def _expert_kernel(be_ref, nu_ref, st_ref, h_hbm, wgu_ref, bgu_ref, wd_ref, bd_ref, o_ref,
                   xbuf0, xbuf1, sem):
    i = pl.program_id(0)
    n_used = nu_ref[0]

    def issue(blk, buf, s):
        base = blk * MOE_ROWS
        for r in range(MOE_ROWS):
            _row_copy(h_hbm, st_ref[base + r], buf, r, s).start()

    def drain(buf, s):
        for r in range(MOE_ROWS):
            _row_copy(h_hbm, 0, buf, r, s).wait()

    @pl.when(jnp.logical_and(i == 0, n_used > 0))
    def _():
        issue(0, xbuf0, sem.at[0])

    def prefetch(cur, cur_sem, nxt, nxt_sem):
        drain(cur, cur_sem)
        issue(jnp.minimum(i + 1, n_used - 1), nxt, nxt_sem)

    def compute(cur, nxt, nxt_sem):
        gu = _dot(cur[...].astype(BF16), wgu_ref[...]) + bgu_ref[...]
        gate = jnp.minimum(gu[:, :D_EXPERT], SWIGLU_LIMIT)
        up = jnp.clip(gu[:, D_EXPERT:], -SWIGLU_LIMIT, SWIGLU_LIMIT)
        act = (up + 1.0) * gate * _sigmoid(SWIGLU_ALPHA * gate)
        o_ref[...] = _dot(act.astype(BF16), wd_ref[...]) + bd_ref[...]

        @pl.when(i == n_used - 1)
        def _():
            drain(nxt, nxt_sem)

    even = i % 2 == 0
    used = i < n_used

    @pl.when(jnp.logical_and(used, even))
    def _():
        prefetch(xbuf0, sem.at[0], xbuf1, sem.at[1])

    @pl.when(jnp.logical_and(used, jnp.logical_not(even)))
    def _():
        prefetch(xbuf1, sem.at[1], xbuf0, sem.at[0])

    @pl.when(jnp.logical_and(even, i + 1 <= n_used))
    def _():
        compute(xbuf0, xbuf1, sem.at[1])

    @pl.when(jnp.logical_and(i % 2 == 1, i + 1 <= n_used))
    def _():
        compute(xbuf1, xbuf0, sem.at[0])

    @pl.when(i >= n_used)
    def _():
        o_ref[...] = jnp.zeros_like(o_ref)


def _experts(h, block_e, n_used, slot_tok, w_gu, b_gu, w_down, b_down, l):
    n_slots = slot_tok.shape[0]
    d = h.shape[1]
    grid_spec = pltpu.PrefetchScalarGridSpec(
        num_scalar_prefetch=3,
        grid=(n_slots // MOE_ROWS,),
        in_specs=[pl.BlockSpec(memory_space=pl.ANY),
                  pl.BlockSpec((None, None, d, 2 * D_EXPERT), lambda i, be, nu, st: (l, be[i], 0, 0)),
                  pl.BlockSpec((None, None, 1, 2 * D_EXPERT), lambda i, be, nu, st: (l, be[i], 0, 0)),
                  pl.BlockSpec((None, None, D_EXPERT, d), lambda i, be, nu, st: (l, be[i], 0, 0)),
                  pl.BlockSpec((None, None, 1, d), lambda i, be, nu, st: (l, be[i], 0, 0))],
        out_specs=pl.BlockSpec((MOE_ROWS, d), lambda i, be, nu, st: (i, 0)),
        scratch_shapes=[pltpu.VMEM((MOE_ROWS, d), F32), pltpu.VMEM((MOE_ROWS, d), F32),
                        pltpu.SemaphoreType.DMA((2,))],
    )
    return pl.pallas_call(
        _expert_kernel,
        grid_spec=grid_spec,
        out_shape=jax.ShapeDtypeStruct((n_slots, d), F32),
        compiler_params=_params("arbitrary"),
        name="experts",
    )(block_e, n_used, slot_tok, h, w_gu, b_gu, w_down, b_down)


COMBINE_TM = 128


def _combine_kernel(dest_ref, ys_hbm, wt_ref, x_ref, gt_ref, o_ref, buf0, buf1, sem):
    i = pl.program_id(0)
    n = pl.num_programs(0)
    tm = COMBINE_TM

    def issue(tile, buf, s):
        base = tile * (tm * TOP_K)
        for r in range(tm):
            for k in range(TOP_K):
                _row_copy(ys_hbm, dest_ref[base + r * TOP_K + k], buf.at[k], r, s).start()

    def drain(buf, s):
        for r in range(tm):
            for k in range(TOP_K):
                _row_copy(ys_hbm, 0, buf.at[k], r, s).wait()

    @pl.when(i == 0)
    def _():
        issue(0, buf0, sem.at[0])

    def prefetch(cur, cur_sem, nxt, nxt_sem):
        drain(cur, cur_sem)
        issue(jnp.minimum(i + 1, n - 1), nxt, nxt_sem)

    def compute(cur, nxt, nxt_sem):
        acc = None
        for k in range(TOP_K):
            term = wt_ref[:, k:k + 1] * cur[k]
            acc = term if acc is None else acc + term
        o_ref[...] = x_ref[...] + gt_ref[...] * acc

        @pl.when(i == n - 1)
        def _():
            drain(nxt, nxt_sem)

    even = i % 2 == 0

    @pl.when(even)
    def _():
        prefetch(buf0, sem.at[0], buf1, sem.at[1])

    @pl.when(jnp.logical_not(even))
    def _():
        prefetch(buf1, sem.at[1], buf0, sem.at[0])

    @pl.when(i % 2 < 1)
    def _():
        compute(buf0, buf1, sem.at[1])

    @pl.when(i % 2 >= 1)
    def _():
        compute(buf1, buf0, sem.at[0])


def _combine(dest, ys, wt, x, mod4, seq):
    m, d = x.shape
    tm = COMBINE_TM
    per_b = seq // tm
    grid_spec = pltpu.PrefetchScalarGridSpec(
        num_scalar_prefetch=1,
        grid=(m // tm,),
        in_specs=[pl.BlockSpec(memory_space=pl.ANY),
                  pl.BlockSpec((tm, LANES), lambda i, ds: (i, 0)),
                  pl.BlockSpec((tm, d), lambda i, ds: (i, 0)),
                  pl.BlockSpec((None, None, 1, d), lambda i, ds: (i // per_b, 5, 0, 0))],
        out_specs=pl.BlockSpec((tm, d), lambda i, ds: (i, 0)),
        scratch_shapes=[pltpu.VMEM((TOP_K, tm, d), F32), pltpu.VMEM((TOP_K, tm, d), F32),
                        pltpu.SemaphoreType.DMA((2,))],
    )
    return pl.pallas_call(
        _combine_kernel,
        grid_spec=grid_spec,
        out_shape=jax.ShapeDtypeStruct((m, d), F32),
        compiler_params=_params("arbitrary"),
        name="moe_combine",
    )(dest, ys, wt, x, mod4)


def _moe(x, g, l, mod4, w_router, b_router, w_gu, b_gu, w_down, b_down, seq):
    m = x.shape[0]
    h, idx, wt, rank, cnt = _router(x, g, l, mod4, w_router, b_router, seq)
    counts = cnt[0, :N_EXPERTS]
    padded = ((counts + MOE_ROWS - 1) // MOE_ROWS) * MOE_ROWS
    ends = jnp.cumsum(padded)
    starts = ends - padded
    dest = (starts[idx[:, :TOP_K]] + rank[:, :TOP_K]).reshape(-1)
    n_slots = m * TOP_K + N_EXPERTS * MOE_ROWS
    n_blocks = n_slots // MOE_ROWS
    slot_tok = jnp.zeros((n_slots,), jnp.int32).at[dest].set(jnp.arange(m * TOP_K, dtype=jnp.int32) // TOP_K)
    block_start = jnp.arange(n_blocks, dtype=jnp.int32) * MOE_ROWS
    block_e = jnp.minimum(jnp.sum(ends[None, :] <= block_start[:, None], axis=1), N_EXPERTS - 1).astype(jnp.int32)
    n_used = (ends[-1:] // MOE_ROWS).astype(jnp.int32)
    ys = _experts(h, block_e, n_used, slot_tok, w_gu, b_gu, w_down, b_down, l)
    return _combine(dest, ys, wt, x, mod4, seq)


def _pad_lanes(v):
    return jnp.pad(v, ((0, 0), (0, LANES - v.shape[-1])))[:, None, :]


def kernel(x, c, w_mod, b_mod, g_norm_mix, g_norm_ffn, w_in, hg_lower_bound, hg_onorm, s5_lambda_re, s5_lambda_im, s5_log_dt, s5_b_re, s5_b_im, s5_c_re, s5_c_im, s5_d, s5_w_glu, s5_b_glu, m2_conv_w, m2_conv_b, m2_dt_bias, m2_a_log, m2_d, m2_norm, rk_mu, rk_w0, rk_w2, rk_a0, rk_a2, rk_g2, rk_k_k, rk_k_a, rk_r_k, rk_ln_w, rk_ln_b, w_branch, w_out, w_router, b_router, w_gu, b_gu, w_down, b_down, g_final):
    batch, seq, d = x.shape
    depth = w_in.shape[0]
    n = batch * seq
    dt0 = N_MAIN
    w_tail = jnp.concatenate([w_in[:, :, dt0 + M2_HEADS:], w_in[:, :, dt0:dt0 + M2_HEADS],
                              jnp.zeros((depth, d, LANES - M2_HEADS), F32)], axis=-1)
    w_branch_b = w_branch.astype(BF16)
    w_out_b = w_out.astype(BF16)
    w_gu_b = w_gu.astype(BF16)
    w_down_b = w_down.astype(BF16)
    w_router_p = jnp.pad(w_router, ((0, 0), (0, 0), (0, LANES - N_EXPERTS)))
    b_router_p = _pad_lanes(b_router)
    b_gu4 = b_gu[:, :, None, :]
    b_down4 = b_down[:, :, None, :]
    lbs = jax.nn.softmax(hg_lower_bound.astype(F32), axis=0)
    lbs = jnp.cumsum(lbs, axis=0) - lbs[0]
    dt_bias_p = _pad_lanes(m2_dt_bias)
    a_log_p = _pad_lanes(m2_a_log)
    m2_d_rep = jnp.repeat(m2_d, M2_HEADDIM, axis=-1)[:, None, :]
    c_pad = jnp.pad(c, ((0, 8 - batch), (0, 0)))

    x2 = x.reshape(n, d)
    for l in range(depth):
        mod = _modulation(c_pad, w_mod, b_mod[:, None, :], l)
        mod4 = mod[:batch].reshape(batch, 6, 1, d)
        h = _norm_modulate(x2, g_norm_mix[:, None, :], l, mod4, 0, 1, seq, BF16)
        proj = _matmul(h, w_in, l, N_MAIN, 512, 1024, F32, "in_proj_main")
        tail = _matmul(h, w_tail, l, N_TAIL, 512, 640, F32, "in_proj_tail")
        ya = _hgrn2(proj, lbs[l][None], hg_onorm[l][None], batch, seq)
        tables = _s5_tables(s5_lambda_re[l], s5_lambda_im[l], s5_log_dt[l], s5_b_re[l], s5_b_im[l],
                            s5_c_re[l], s5_c_im[l], seq // S5_T)
        yb = _s5(proj, tables, s5_d[l][None], s5_w_glu[l].astype(BF16), s5_b_glu[l][None], batch, seq)
        yc = _mamba2(proj, tail, m2_conv_w[l], m2_conv_b[l][None], dt_bias_p[l], a_log_p[l], m2_d_rep[l],
                     m2_norm[l][None], batch, seq)
        yd = _rwkv7(tail, rk_mu[l][None], rk_w0[l][None], rk_w2[l], rk_a0[l][None], rk_a2[l], rk_g2[l],
                    rk_k_k[l][None], rk_k_a[l][None], rk_r_k[l].reshape(1, BRANCH), rk_ln_w[l][None],
                    rk_ln_b[l][None], batch, seq)
        merged = _merge([ya, yb, yc, yd], proj, w_branch_b, l)
        x2 = _out_proj_residual(merged, w_out_b, l, x2, mod4, 2, seq)
        x2 = _moe(x2, g_norm_ffn[:, None, :], l, mod4, w_router_p, b_router_p, w_gu_b, b_gu4,
                  w_down_b, b_down4, seq)
    return _final_norm(x2, g_final[None]).reshape(batch, seq, d)
```

```python
import functools
import math

import jax
import jax.numpy as jnp
from jax import lax
from jax.experimental import pallas as pl
from jax.experimental.pallas import tpu as pltpu

F32 = jnp.float32
BF16 = jnp.bfloat16
HI = lax.Precision.HIGHEST

D_MODEL = 2048
CHUNK = 64
BRANCH = 512
HG_HEADS, HG_DK = 4, 128
S5_CH, S5_GROUPS, S5_STATE = 16, 32, 64
S5_T = 16
M2_HEADS, M2_HEADDIM, M2_GROUPS, M2_STATE, M2_CONV, M2_XBC = 8, 64, 2, 128, 4, 1024
RK_HEADS, RK_HEADSIZE = 8, 64
RK_DECAY_LORA, RK_A_LORA, RK_GATE_LORA, RK_WIDTH = 64, 64, 128, 1792
RK_GN_EPS = 64e-5
GATE_IN = 4 * D_MODEL
N_MAIN = GATE_IN + 4 * BRANCH + BRANCH + BRANCH + M2_XBC
N_TAIL = RK_WIDTH + 128
N_EXPERTS, TOP_K, D_EXPERT = 32, 4, 768
SWIGLU_LIMIT, SWIGLU_ALPHA = 7.0, 1.702
MOE_ROWS = 256
NORM_EPS = 1e-6
LANES = 128
VMEM_LIMIT = 48 * 1024 * 1024


def _dot(a, b, prec=None):
    return jnp.dot(a, b, preferred_element_type=F32, precision=prec)


def _dot_nt(a, b, prec=None):
    return lax.dot_general(a, b, (((1,), (1,)), ((), ())), preferred_element_type=F32, precision=prec)


def _dot_tn(a, b, prec=None):
    return lax.dot_general(a, b, (((0,), (0,)), ((), ())), preferred_element_type=F32, precision=prec)


def _bdot(a, b):
    return _dot(a.astype(BF16), b.astype(BF16))


def _bdot_nt(a, b):
    return _dot_nt(a.astype(BF16), b.astype(BF16))


def _bdot_tn(a, b):
    return _dot_tn(a.astype(BF16), b.astype(BF16))


def _sigmoid(x):
    return 1.0 / (1.0 + jnp.exp(-x))


def _silu(x):
    return x * _sigmoid(x)


def _softplus(x):
    return jnp.maximum(x, 0.0) + jnp.log(1.0 + jnp.exp(-jnp.abs(x)))


def _tril(n, k=0):
    r = lax.broadcasted_iota(jnp.int32, (n, n), 0)
    c = lax.broadcasted_iota(jnp.int32, (n, n), 1)
    return (c - r) <= k


def _params(*sem):
    return pltpu.CompilerParams(dimension_semantics=sem, vmem_limit_bytes=VMEM_LIMIT)


def _mm_kernel(x_ref, w_ref, o_ref, wbf_ref):
    @pl.when(pl.program_id(1) == 0)
    def _():
        wbf_ref[...] = w_ref[...].astype(BF16)

    o_ref[...] = _dot(x_ref[...], wbf_ref[...]).astype(o_ref.dtype)


def _mod_kernel(c_ref, w_ref, b_ref, o_ref):
    o_ref[...] = _dot(_silu(c_ref[...]).astype(BF16), w_ref[...].astype(BF16)) + b_ref[...]


def _mm_resid_kernel(x_ref, w_ref, r_ref, g_ref, o_ref):
    o_ref[...] = r_ref[...] + g_ref[...] * _dot(x_ref[...], w_ref[...])


def _matmul(x, w, l, n_cols, tm, tn, out_dtype, name):
    m, k = x.shape
    return pl.pallas_call(
        _mm_kernel,
        grid=(n_cols // tn, m // tm),
        in_specs=[pl.BlockSpec((tm, k), lambda j, i: (i, 0)),
                  pl.BlockSpec((None, k, tn), lambda j, i: (l, 0, j))],
        out_specs=pl.BlockSpec((tm, tn), lambda j, i: (i, j)),
        out_shape=jax.ShapeDtypeStruct((m, n_cols), out_dtype),
        scratch_shapes=[pltpu.VMEM((k, tn), BF16)],
        compiler_params=_params("parallel", "arbitrary"),
        name=name,
    )(x, w)


def _modulation(cond, w_mod, b_mod, l):
    m, k = cond.shape
    n = w_mod.shape[-1]
    tn = 1024
    return pl.pallas_call(
        _mod_kernel,
        grid=(n // tn,),
        in_specs=[pl.BlockSpec((m, k), lambda j: (0, 0)),
                  pl.BlockSpec((None, k, tn), lambda j: (l, 0, j)),
                  pl.BlockSpec((None, 1, tn), lambda j: (l, 0, j))],
        out_specs=pl.BlockSpec((m, tn), lambda j: (0, j)),
        out_shape=jax.ShapeDtypeStruct((m, n), F32),
        compiler_params=_params("parallel"),
        name="modulation",
    )(cond, w_mod, b_mod)


def _out_proj_residual(merged, w_out, l, x, mod4, gate_idx, seq):
    m, k = merged.shape
    tm, tn = 512, 512
    per_b = seq // tm
    return pl.pallas_call(
        _mm_resid_kernel,
        grid=(D_MODEL // tn, m // tm),
        in_specs=[pl.BlockSpec((tm, k), lambda j, i: (i, 0)),
                  pl.BlockSpec((None, k, tn), lambda j, i: (l, 0, j)),
                  pl.BlockSpec((tm, tn), lambda j, i: (i, j)),
                  pl.BlockSpec((None, None, 1, tn), lambda j, i: (i // per_b, gate_idx, 0, j))],
        out_specs=pl.BlockSpec((tm, tn), lambda j, i: (i, j)),
        out_shape=jax.ShapeDtypeStruct((m, D_MODEL), F32),
        compiler_params=_params("parallel", "parallel"),
        name="out_proj_residual",
    )(merged, w_out, x, mod4)


def _rms(x, g):
    return x * lax.rsqrt(jnp.mean(x * x, axis=-1, keepdims=True) + NORM_EPS) * g


def _normmod_kernel(x_ref, g_ref, sc_ref, sh_ref, o_ref):
    y = _rms(x_ref[...], g_ref[...])
    o_ref[...] = (y * (1.0 + sc_ref[...]) + sh_ref[...]).astype(o_ref.dtype)


def _norm_modulate(x, g, l, mod4, shift_idx, scale_idx, seq, out_dtype):
    m, d = x.shape
    ts = 512
    per_b = seq // ts
    return pl.pallas_call(
        _normmod_kernel,
        grid=(m // ts,),
        in_specs=[pl.BlockSpec((ts, d), lambda i: (i, 0)),
                  pl.BlockSpec((None, 1, d), lambda i: (l, 0, 0)),
                  pl.BlockSpec((None, None, 1, d), lambda i: (i // per_b, scale_idx, 0, 0)),
                  pl.BlockSpec((None, None, 1, d), lambda i: (i // per_b, shift_idx, 0, 0))],
        out_specs=pl.BlockSpec((ts, d), lambda i: (i, 0)),
        out_shape=jax.ShapeDtypeStruct((m, d), out_dtype),
        compiler_params=_params("parallel"),
        name="norm_modulate",
    )(x, g, mod4, mod4)


def _final_norm_kernel(x_ref, g_ref, o_ref):
    o_ref[...] = _rms(x_ref[...], g_ref[...])


def _final_norm(x, g):
    m, d = x.shape
    ts = 512
    return pl.pallas_call(
        _final_norm_kernel,
        grid=(m // ts,),
        in_specs=[pl.BlockSpec((ts, d), lambda i: (i, 0)), pl.BlockSpec((1, d), lambda i: (0, 0))],
        out_specs=pl.BlockSpec((ts, d), lambda i: (i, 0)),
        out_shape=jax.ShapeDtypeStruct((m, d), F32),
        compiler_params=_params("parallel"),
        name="final_norm",
    )(x, g)


HG_SUB = 16


def _hgrn2_kernel(q_ref, f_ref, i_ref, g_ref, lb_ref, on_ref, o_ref, state_ref):
    @pl.when(pl.program_id(1) == 0)
    def _():
        state_ref[...] = jnp.zeros_like(state_ref)

    sub_mask = _tril(HG_SUB)
    lb = lb_ref[...]
    forget = lb + (1.0 - lb) * _sigmoid(f_ref[...])
    k = 1.0 - forget
    q = _silu(q_ref[...])
    v = i_ref[...]
    cum = _dot(_tril(CHUNK).astype(F32), jnp.log(forget), HI)
    last = cum[CHUNK - 1:CHUNK, :]
    q_dec = q * jnp.exp(cum)
    k_end = k * jnp.exp(last - cum)
    heads = range(HG_HEADS)
    sls = [slice(h * HG_DK, (h + 1) * HG_DK) for h in heads]
    sts = [state_ref[h] for h in heads]
    o_inter = [_bdot_nt(q_dec[:, sls[h]], sts[h]) for h in heads]
    for h in heads:
        state_ref[h] = sts[h] * jnp.exp(last[:, sls[h]]) + _bdot_tn(v[:, sls[h]], k_end[:, sls[h]])
    rows = [[None] * (CHUNK // HG_SUB) for _ in heads]
    for i in range(CHUNK // HG_SUB):
        r0 = i * HG_SUB
        if i > 0:
            ci = cum[r0 - 1:r0, :]
            qp = q[r0:r0 + HG_SUB] * jnp.exp(cum[r0:r0 + HG_SUB] - ci)
            kp = k[:r0] * jnp.exp(ci - cum[:r0])
            sc = [_bdot_nt(qp[:, sls[h]], kp[:, sls[h]]) for h in heads]
            off = [_bdot(sc[h], v[:r0, sls[h]]) for h in heads]
        for h in heads:
            sl = sls[h]
            qi = q[r0:r0 + HG_SUB, sl]
            cumi = cum[r0:r0 + HG_SUB, sl]
            oi = o_inter[h][r0:r0 + HG_SUB]
            if i > 0:
                oi = oi + off[h]
            for s in range(HG_SUB):
                diff = jnp.where(sub_mask[:, s:s + 1], cumi - cumi[s:s + 1, :], -jnp.inf)
                w = jnp.sum(qi * jnp.exp(diff) * k[r0 + s:r0 + s + 1, sl], axis=-1, keepdims=True)
                oi = oi + w * v[r0 + s:r0 + s + 1, sl]
            rows[h][i] = oi
    outs = [_rms(jnp.concatenate(rows[h], axis=0), on_ref[...]) for h in heads]
    o_ref[...] = jnp.concatenate(outs, axis=-1) * _silu(g_ref[...])


def _hgrn2(proj, lb, onorm, batch, seq):
    nc = seq // CHUNK
    col0 = GATE_IN // BRANCH
    spec = lambda off: pl.BlockSpec((CHUNK, BRANCH), lambda b, c: (b * nc + c, col0 + off))
    vec = lambda n: pl.BlockSpec((1, n), lambda b, c: (0, 0))
    return pl.pallas_call(
        _hgrn2_kernel,
        grid=(batch, nc),
        in_specs=[spec(0), spec(1), spec(2), spec(3), vec(BRANCH), vec(HG_DK)],
        out_specs=pl.BlockSpec((CHUNK, BRANCH), lambda b, c: (b * nc + c, 0)),
        out_shape=jax.ShapeDtypeStruct((batch * seq, BRANCH), F32),
        scratch_shapes=[pltpu.VMEM((HG_HEADS, HG_DK, HG_DK), F32)],
        compiler_params=_params("parallel", "arbitrary"),
        name="hgrn2",
    )(proj, proj, proj, proj, lb, onorm)


def _mamba2_kernel(z_ref, xbc_ref, dt_ref, cw_ref, cb_ref, dtb_ref, alog_ref, d_ref, ng_ref,
                   o_ref, ext_ref, state_ref):
    @pl.when(pl.program_id(1) == 0)
    def _():
        ext_ref[0:8, :] = jnp.zeros((8, M2_XBC), F32)
        state_ref[...] = jnp.zeros_like(state_ref)

    ext_ref[8:8 + CHUNK, :] = xbc_ref[...]
    conv = cb_ref[...]
    for j in range(M2_CONV):
        conv = conv + cw_ref[j:j + 1, :] * ext_ref[8 - (M2_CONV - 1) + j:8 - (M2_CONV - 1) + j + CHUNK, :]
    ext_ref[0:8, :] = xbc_ref[CHUNK - 8:CHUNK, :]
    xa = _silu(conv)
    gn = M2_GROUPS * M2_STATE
    xs = xa[:, :BRANCH]
    dt = _softplus(dt_ref[...] + dtb_ref[...])
    adt = -jnp.exp(alog_ref[...]) * dt
    tri = _tril(CHUNK)
    cum = _dot(tri.astype(F32), adt, HI)
    cum_t = cum.T
    cb = []
    for g in range(M2_GROUPS):
        bm = xa[:, BRANCH + g * M2_STATE:BRANCH + (g + 1) * M2_STATE]
        cm = xa[:, BRANCH + gn + g * M2_STATE:BRANCH + gn + (g + 1) * M2_STATE]
        cb.append((bm, cm, _bdot_nt(cm, bm)))
    ys = []
    for h in range(M2_HEADS):
        bm, cm, cbg = cb[h // (M2_HEADS // M2_GROUPS)]
        cum_h = cum[:, h:h + 1]
        lmat = jnp.exp(jnp.where(tri, cum_h - cum_t[h:h + 1, :], -jnp.inf))
        x_h = xs[:, h * M2_HEADDIM:(h + 1) * M2_HEADDIM]
        xdt = x_h * dt[:, h:h + 1]
        st = state_ref[h]
        y = _bdot(cbg * lmat, xdt) + jnp.exp(cum_h) * _bdot(cm, st)
        last = cum[CHUNK - 1:CHUNK, h:h + 1]
        state_ref[h] = jnp.exp(last) * st + _bdot_tn(bm * jnp.exp(last - cum_h), xdt)
        ys.append(y)
    y = jnp.concatenate(ys, axis=-1) + d_ref[...] * xs
    y = y * _silu(z_ref[...])
    gw = BRANCH // M2_GROUPS
    outs = [_rms(y[:, g * gw:(g + 1) * gw], ng_ref[:, g * gw:(g + 1) * gw]) for g in range(M2_GROUPS)]
    o_ref[...] = jnp.concatenate(outs, axis=-1)


def _mamba2(proj, tail, conv_w, conv_b, dt_bias, a_log, d_rep, norm_g, batch, seq):
    nc = seq // CHUNK
    zcol = (GATE_IN + 5 * BRANCH) // BRANCH
    xcol = (GATE_IN + 6 * BRANCH) // M2_XBC
    vec = lambda r, n: pl.BlockSpec((r, n), lambda b, c: (0, 0))
    return pl.pallas_call(
        _mamba2_kernel,
        grid=(batch, nc),
        in_specs=[pl.BlockSpec((CHUNK, BRANCH), lambda b, c: (b * nc + c, zcol)),
                  pl.BlockSpec((CHUNK, M2_XBC), lambda b, c: (b * nc + c, xcol)),
                  pl.BlockSpec((CHUNK, LANES), lambda b, c: (b * nc + c, RK_WIDTH // LANES)),
                  vec(M2_CONV, M2_XBC), vec(1, M2_XBC), vec(1, LANES), vec(1, LANES),
                  vec(1, BRANCH), vec(1, BRANCH)],
        out_specs=pl.BlockSpec((CHUNK, BRANCH), lambda b, c: (b * nc + c, 0)),
        out_shape=jax.ShapeDtypeStruct((batch * seq, BRANCH), F32),
        scratch_shapes=[pltpu.VMEM((8 + CHUNK, M2_XBC), F32),
                        pltpu.VMEM((M2_HEADS, M2_STATE, M2_HEADDIM), F32)],
        compiler_params=_params("parallel", "arbitrary"),
        name="mamba2",
    )(proj, proj, tail, conv_w, conv_b, dt_bias, a_log, d_rep, norm_g)


def _unit_lower_inverses(mats):
    n = mats[0].shape[0]
    eye = (lax.broadcasted_iota(jnp.int32, (n, n), 0) == lax.broadcasted_iota(jnp.int32, (n, n), 1)).astype(F32)
    invs = [eye - a for a in mats]
    pows = list(mats)
    k = 2
    while k < n:
        pows = [_bdot(p, p) for p in pows]
        invs = [_bdot(inv, eye + p) for inv, p in zip(invs, pows)]
        k *= 2
    return invs


def _rwkv7_kernel(p_ref, mu_ref, w0_ref, w2_ref, a0_ref, a2_ref, g2_ref, kk_ref, ka_ref, rk_ref,
                  lnw_ref, lnb_ref, o_ref, ext_ref, state_ref):
    @pl.when(pl.program_id(1) == 0)
    def _():
        ext_ref[0:8, :] = jnp.zeros((8, RK_WIDTH), F32)
        state_ref[...] = jnp.zeros_like(state_ref)

    p = p_ref[...]
    ext_ref[8:8 + CHUNK, :] = p
    prev = ext_ref[7:7 + CHUNK, :]
    ext_ref[0:8, :] = p_ref[CHUNK - 8:CHUNK, :]
    pm = p + (prev - p) * mu_ref[...]
    b3 = 3 * BRANCH
    r = pm[:, :BRANCH]
    k = pm[:, BRANCH:2 * BRANCH]
    v = pm[:, 2 * BRANCH:b3]
    w_lo = pm[:, b3:b3 + RK_DECAY_LORA]
    a_lo = pm[:, b3 + RK_DECAY_LORA:b3 + RK_DECAY_LORA + RK_A_LORA]
    g_lo = pm[:, b3 + RK_DECAY_LORA + RK_A_LORA:]
    w = -_softplus(-(w0_ref[...] + _bdot(jnp.tanh(w_lo), w2_ref[...]))) - 0.5
    logw = -jnp.exp(w)
    a = _sigmoid(a0_ref[...] + _bdot(a_lo, a2_ref[...]))
    g = _bdot(_sigmoid(g_lo), g2_ref[...])
    kk = k * kk_ref[...]
    k = k * (1.0 + (a - 1.0) * ka_ref[...])
    c, hs = CHUNK, RK_HEADSIZE
    row = lax.broadcasted_iota(jnp.int32, (2 * c, 2 * c), 0)
    col = lax.broadcasted_iota(jnp.int32, (2 * c, 2 * c), 1)
    quad_mask = (col & (c - 1)) < (row & (c - 1)) + (row >= c).astype(jnp.int32)
    cum = _dot(_tril(c).astype(F32), logw, HI)
    last = cum[c - 1:c, :]
    gam = jnp.exp(cum)
    ginv = jnp.exp(-cum)
    to_end = jnp.exp(last - cum)
    gam_prev = jnp.exp(cum - logw)
    heads = range(RK_HEADS)
    sls = [slice(h * hs, (h + 1) * hs) for h in heads]
    kap = []
    for sl in sls:
        kh = kk[:, sl]
        kap.append(kh / jnp.maximum(jnp.sqrt(jnp.sum(kh * kh, axis=-1, keepdims=True)), 1e-12))
    beta = [kap[h] * a[:, sls[h]] for h in heads]
    lhs = [jnp.concatenate([kap[h] * gam_prev[:, sls[h]], r[:, sls[h]] * gam[:, sls[h]]], axis=0)
           for h in heads]
    rhs = [jnp.concatenate([beta[h] * ginv[:, sls[h]], k[:, sls[h]] * ginv[:, sls[h]]], axis=0)
           for h in heads]
    quad = [jnp.where(quad_mask, _bdot_nt(lhs[h], rhs[h]), 0.0) for h in heads]
    invs = _unit_lower_inverses([q[:c, :c] for q in quad])
    vs = [v[:, sl] for sl in sls]
    akv = [_bdot(quad[h][:, c:], vs[h]) for h in heads]
    sts = [state_ref[h] for h in heads]
    from_state = [_bdot_nt(lhs[h], sts[h]) for h in heads]
    us = [_bdot(invs[h], -(from_state[h][:c] + akv[h][:c])) for h in heads]
    ys = [from_state[h][c:] + akv[h][c:] + _bdot(quad[h][c:, :c], us[h]) for h in heads]
    for h in heads:
        sl = sls[h]
        upd = _bdot_tn(jnp.concatenate([us[h], vs[h]], axis=0),
                       jnp.concatenate([beta[h] * to_end[:, sl], k[:, sl] * to_end[:, sl]], axis=0))
        state_ref[h] = sts[h] * gam[c - 1:c, sl] + upd
    outs = []
    for h in heads:
        sl = sls[h]
        mean = jnp.mean(ys[h], axis=-1, keepdims=True)
        yc = ys[h] - mean
        var = jnp.mean(yc * yc, axis=-1, keepdims=True)
        yn = yc * lax.rsqrt(var + RK_GN_EPS) * lnw_ref[:, sl] + lnb_ref[:, sl]
        bonus = jnp.sum(r[:, sl] * k[:, sl] * rk_ref[:, sl], axis=-1, keepdims=True) * vs[h]
        outs.append(yn + bonus)
    o_ref[...] = jnp.concatenate(outs, axis=-1) * g


def _rwkv7(tail, mu, w0, w2, a0, a2, g2, k_k, k_a, r_k, ln_w, ln_b, batch, seq):
    nc = seq // CHUNK
    vec = lambda r, n: pl.BlockSpec((r, n), lambda b, c: (0, 0))
    return pl.pallas_call(
        _rwkv7_kernel,
        grid=(batch, nc),
        in_specs=[pl.BlockSpec((CHUNK, RK_WIDTH), lambda b, c: (b * nc + c, 0)),
                  vec(1, RK_WIDTH), vec(1, BRANCH), vec(RK_DECAY_LORA, BRANCH), vec(1, BRANCH),
                  vec(RK_A_LORA, BRANCH), vec(RK_GATE_LORA, BRANCH), vec(1, BRANCH), vec(1, BRANCH),
                  vec(1, BRANCH), vec(1, BRANCH), vec(1, BRANCH)],
        out_specs=pl.BlockSpec((CHUNK, BRANCH), lambda b, c: (b * nc + c, 0)),
        out_shape=jax.ShapeDtypeStruct((batch * seq, BRANCH), F32),
        scratch_shapes=[pltpu.VMEM((8 + CHUNK, RK_WIDTH), F32),
                        pltpu.VMEM((RK_HEADS, RK_HEADSIZE, RK_HEADSIZE), F32)],
        compiler_params=_params("parallel", "arbitrary"),
        name="rwkv7",
    )(tail, mu, w0, w2, a0, a2, g2, k_k, k_a, r_k, ln_w, ln_b)


def _s5_tables(lam_re, lam_im, log_dt, b_re, b_im, c_re, c_im, n_chunks):
    t = S5_T
    g, p, i = b_re.shape
    dt = jnp.exp(log_dt)[:, None]

    def apow(n):
        e = n.astype(F32)[..., None, None]
        mag = jnp.exp(e * lam_re * dt)
        return mag * jnp.cos(e * lam_im * dt), mag * jnp.sin(e * lam_im * dt)

    ab_re, ab_im = apow(jnp.ones((), F32))
    den = lam_re * lam_re + lam_im * lam_im
    nr = ab_re - 1.0
    coef_re = ((nr * lam_re + ab_im * lam_im) / den)[..., None]
    coef_im = ((ab_im * lam_re - nr * lam_im) / den)[..., None]
    bt_re = coef_re * b_re - coef_im * b_im
    bt_im = coef_re * b_im + coef_im * b_re
    ar, ai = apow(jnp.arange(t + 1))
    cr, ci = c_re[None], c_im[None]
    ca_re = cr * ar[:, :, None, :] - ci * ai[:, :, None, :]
    ca_im = cr * ai[:, :, None, :] + ci * ar[:, :, None, :]
    kern = (jnp.einsum('tgop,gpi->tgoi', ca_re[:t], bt_re, precision=HI)
            - jnp.einsum('tgop,gpi->tgoi', ca_im[:t], bt_im, precision=HI))
    lag = jnp.arange(t)[None, :] - jnp.arange(t)[:, None]
    toep = jnp.where((lag >= 0)[:, :, None, None, None], kern[jnp.maximum(lag, 0)], 0.0)
    toep = toep.transpose(2, 0, 4, 1, 3).reshape(g, t * i, t * i)
    br, bi = ar[:t][::-1], ai[:t][::-1]
    bs_re = br[..., None] * bt_re - bi[..., None] * bt_im
    bs_im = br[..., None] * bt_im + bi[..., None] * bt_re
    bst = jnp.concatenate([bs_re, bs_im], axis=2).transpose(1, 0, 3, 2).reshape(g, t * i, 2 * p)
    cst = jnp.concatenate([ca_re[1:], -ca_im[1:]], axis=3)
    cst = cst.transpose(1, 3, 0, 2).reshape(g, 2 * p, t * i)
    levels = max(1, int(math.log2(n_chunks)))
    pr, pi = apow(t * (2 ** jnp.arange(levels)))
    m1 = jnp.concatenate([pr, pr], axis=-1)[:, :, None, :]
    m2 = jnp.concatenate([-pi, pi], axis=-1)[:, :, None, :]
    return toep.astype(BF16), bst.astype(BF16), cst.astype(BF16), m1, m2


def _s5_kernel(n_chunks, u_ref, toep_ref, bst_ref, cst_ref, m1_ref, m2_ref, y_ref):
    u = u_ref[...]
    y = _dot(u, toep_ref[...])
    x = _dot(u, bst_ref[...])
    rows = lax.broadcasted_iota(jnp.int32, x.shape, 0) & (n_chunks - 1)
    half = x.shape[1] // 2

    def shifted(val, d):
        return jnp.where(rows >= d, pltpu.roll(val, d, axis=0), 0.0)

    d = 1
    lvl = 0
    while d < n_chunks:
        xs = shifted(x, d)
        x = x + xs * m1_ref[lvl] + pltpu.roll(xs, half, axis=1) * m2_ref[lvl]
        d *= 2
        lvl += 1
    y_ref[...] = y + _dot(shifted(x, 1).astype(BF16), cst_ref[...])


def _s5_scan(u_g, toep, bst, cst, m1, m2, n_chunks):
    g, r, w = u_g.shape
    p2 = bst.shape[-1]
    levels = m1.shape[0]
    return pl.pallas_call(
        functools.partial(_s5_kernel, n_chunks),
        grid=(g,),
        in_specs=[pl.BlockSpec((None, r, w), lambda i: (i, 0, 0)),
                  pl.BlockSpec((None, w, w), lambda i: (i, 0, 0)),
                  pl.BlockSpec((None, w, p2), lambda i: (i, 0, 0)),
                  pl.BlockSpec((None, p2, w), lambda i: (i, 0, 0)),
                  pl.BlockSpec((levels, None, 1, p2), lambda i: (0, i, 0, 0)),
                  pl.BlockSpec((levels, None, 1, p2), lambda i: (0, i, 0, 0))],
        out_specs=pl.BlockSpec((None, r, w), lambda i: (i, 0, 0)),
        out_shape=jax.ShapeDtypeStruct((g, r, w), F32),
        compiler_params=_params("parallel"),
        name="s5_scan",
    )(u_g, toep, bst, cst, m1, m2)


def _s5_post_kernel(y_ref, u_ref, d_ref, w_ref, b_ref, o_ref):
    y = y_ref[...] + d_ref[...] * u_ref[...]
    y = 0.5 * y * (1.0 + jnp.tanh(math.sqrt(2.0 / math.pi) * (y + 0.044715 * (y * y * y))))
    o_ref[...] = y * _sigmoid(_bdot(y, w_ref[...]) + b_ref[...])


def _s5_post(y_ssm, proj, d_skip, w_glu, b_glu):
    m = y_ssm.shape[0]
    tm = min(512, m)
    ucol = (GATE_IN + 4 * BRANCH) // BRANCH
    return pl.pallas_call(
        _s5_post_kernel,
        grid=(m // tm,),
        in_specs=[pl.BlockSpec((tm, BRANCH), lambda i: (i, 0)),
                  pl.BlockSpec((tm, BRANCH), lambda i: (i, ucol)),
                  pl.BlockSpec((1, BRANCH), lambda i: (0, 0)),
                  pl.BlockSpec((BRANCH, BRANCH), lambda i: (0, 0)),
                  pl.BlockSpec((1, BRANCH), lambda i: (0, 0))],
        out_specs=pl.BlockSpec((tm, BRANCH), lambda i: (i, 0)),
        out_shape=jax.ShapeDtypeStruct((m, BRANCH), F32),
        compiler_params=_params("parallel"),
        name="s5_post",
    )(y_ssm, proj, d_skip, w_glu, b_glu)


def _s5(proj, tables, d_skip, w_glu, b_glu, batch, seq):
    n = batch * seq
    nc = seq // S5_T
    u0 = GATE_IN + 4 * BRANCH
    u = proj[:, u0:u0 + BRANCH].reshape(batch, nc, S5_T, S5_GROUPS, S5_CH)
    u_g = u.transpose(3, 0, 1, 2, 4).reshape(S5_GROUPS, batch * nc, S5_T * S5_CH).astype(BF16)
    y_g = _s5_scan(u_g, *tables, nc)
    y = y_g.reshape(S5_GROUPS, batch, nc, S5_T, S5_CH).transpose(1, 2, 3, 0, 4).reshape(n, BRANCH)
    return _s5_post(y, proj, d_skip, w_glu, b_glu)


def _merge_kernel(ya_ref, yb_ref, yc_ref, yd_ref, ga_ref, gb_ref, gc_ref, gd_ref,
                  wa_ref, wb_ref, wc_ref, wd_ref, o_ref):
    acc = None
    for y_ref, g_ref, w_ref in ((ya_ref, ga_ref, wa_ref), (yb_ref, gb_ref, wb_ref),
                                (yc_ref, gc_ref, wc_ref), (yd_ref, gd_ref, wd_ref)):
        term = _sigmoid(g_ref[...]) * _dot(y_ref[...].astype(BF16), w_ref[...])
        acc = term if acc is None else acc + term
    o_ref[...] = acc.astype(o_ref.dtype)


def _merge(ys, proj, w_branch, l):
    m = ys[0].shape[0]
    tm = 256
    yspec = pl.BlockSpec((tm, BRANCH), lambda i: (i, 0))
    gspec = lambda n: pl.BlockSpec((tm, D_MODEL), lambda i: (i, n))
    wspec = lambda n: pl.BlockSpec((None, None, BRANCH, D_MODEL), lambda i: (l, n, 0, 0))
    return pl.pallas_call(
        _merge_kernel,
        grid=(m // tm,),
        in_specs=[yspec] * 4 + [gspec(n) for n in range(4)] + [wspec(n) for n in range(4)],
        out_specs=pl.BlockSpec((tm, D_MODEL), lambda i: (i, 0)),
        out_shape=jax.ShapeDtypeStruct((m, D_MODEL), BF16),
        compiler_params=_params("parallel"),
        name="branch_merge",
    )(*ys, proj, proj, proj, proj, w_branch, w_branch, w_branch, w_branch)


ROUTER_TM = 256
HALF_D = D_MODEL // 2


def _pack_bf16_halves(lo, hi):
    lo_bits = pltpu.bitcast(lo.astype(BF16).astype(F32), jnp.uint32)
    hi_bits = pltpu.bitcast(hi.astype(BF16).astype(F32), jnp.uint32)
    return (hi_bits & jnp.uint32(0xFFFF0000)) | (lo_bits >> 16)


def _pack_bf16_pairs(v):
    return _pack_bf16_halves(v[:, :HALF_D], v[:, HALF_D:])


def _unpack_bf16_pairs(p):
    lo = pltpu.bitcast(p << 16, F32)
    hi = pltpu.bitcast(p & jnp.uint32(0xFFFF0000), F32)
    return lo, hi


def _router_kernel(x_ref, g_ref, sc_ref, sh_ref, wr_ref, br_ref,
                   h_ref, idx_ref, wt_ref, rank_ref, cnt_ref, carry_ref):
    @pl.when(pl.program_id(0) == 0)
    def _():
        carry_ref[...] = jnp.zeros_like(carry_ref)

    h = _rms(x_ref[...], g_ref[...]) * (1.0 + sc_ref[...]) + sh_ref[...]
    h_ref[...] = _pack_bf16_pairs(h)
    tm = h.shape[0]
    lane = lax.broadcasted_iota(jnp.int32, (tm, LANES), 1)
    logits = _dot(h, wr_ref[...], HI) + br_ref[...]
    masked = jnp.where(lane < N_EXPERTS, logits, -jnp.inf)
    vals, hots, idxs = [], [], []
    for _ in range(TOP_K):
        m = jnp.max(masked, axis=-1, keepdims=True)
        idx = jnp.min(jnp.where(masked == m, lane, LANES), axis=-1, keepdims=True)
        hot = lane == idx
        masked = jnp.where(hot, -jnp.inf, masked)
        vals.append(m)
        hots.append(hot)
        idxs.append(idx)
    exps = [jnp.exp(v - vals[0]) for v in vals]
    denom = exps[0] + exps[1] + exps[2] + exps[3]
    multi = sum(hot.astype(F32) for hot in hots)
    before = _dot(_tril(tm, -1).astype(BF16), multi.astype(BF16)) + carry_ref[...]
    carry_ref[...] = carry_ref[...] + jnp.sum(multi, axis=0, keepdims=True)
    idx_out = jnp.zeros((tm, LANES), jnp.int32)
    wt_out = jnp.zeros((tm, LANES), F32)
    rank_out = jnp.zeros((tm, LANES), F32)
    for k in range(TOP_K):
        rank = jnp.sum(jnp.where(hots[k], before, 0.0), axis=-1, keepdims=True)
        idx_out = jnp.where(lane == k, idxs[k], idx_out)
        wt_out = jnp.where(lane == k, exps[k] / denom, wt_out)
        rank_out = jnp.where(lane == k, rank, rank_out)
    idx_ref[...] = idx_out
    wt_ref[...] = wt_out
    rank_ref[...] = rank_out.astype(jnp.int32)
    cnt_ref[...] = carry_ref[...].astype(jnp.int32)


def _router(x, g, l, mod4, w_router, b_router, seq):
    m, d = x.shape
    tm = min(ROUTER_TM, m)
    per_b = seq // tm
    tok = lambda dt: jax.ShapeDtypeStruct((m, LANES), dt)
    tspec = pl.BlockSpec((tm, LANES), lambda i: (i, 0))
    return pl.pallas_call(
        _router_kernel,
        grid=(m // tm,),
        in_specs=[pl.BlockSpec((tm, d), lambda i: (i, 0)),
                  pl.BlockSpec((None, 1, d), lambda i: (l, 0, 0)),
                  pl.BlockSpec((None, None, 1, d), lambda i: (i // per_b, 4, 0, 0)),
                  pl.BlockSpec((None, None, 1, d), lambda i: (i // per_b, 3, 0, 0)),
                  pl.BlockSpec((None, d, LANES), lambda i: (l, 0, 0)),
                  pl.BlockSpec((None, 1, LANES), lambda i: (l, 0, 0))],
        out_specs=[pl.BlockSpec((tm, HALF_D), lambda i: (i, 0)), tspec, tspec, tspec,
                   pl.BlockSpec((1, LANES), lambda i: (0, 0))],
        out_shape=[jax.ShapeDtypeStruct((m, HALF_D), jnp.uint32), tok(jnp.int32), tok(F32), tok(jnp.int32),
                   jax.ShapeDtypeStruct((1, LANES), jnp.int32)],
        scratch_shapes=[pltpu.VMEM((1, LANES), F32)],
        compiler_params=_params("arbitrary"),
        name="router",
    )(x, g, mod4, mod4, w_router, b_router)


def _row_copy(src_hbm, row, buf, r, sem):
    return pltpu.make_async_copy(src_hbm.at[pl.ds(row, 1), :], buf.at[pl.ds(r, 1), :], sem)


ISSUE_SPLIT = (64, 160)


def _expert_kernel(be_ref, nu_ref, st_ref, h_hbm, wgu_ref, bgu_ref, wd_ref, bd_ref, o_ref,
                   xbuf0, xbuf1, x_ref, act_ref, sem):
    i = pl.program_id(0)
    n_used = nu_ref[0]

    def issue(base, r0, count, buf, s):
        for r in range(count):
            _row_copy(h_hbm, st_ref[base + r0 + r], buf, r0 + r, s).start()

    def drain(buf, s):
        for r in range(MOE_ROWS):
            _row_copy(h_hbm, 0, buf, r, s).wait()

    @pl.when(jnp.logical_and(i == 0, n_used > 0))
    def _():
        issue(0, 0, MOE_ROWS, xbuf0, sem.at[0])

    def step(cur, cur_sem, nxt, nxt_sem):
        drain(cur, cur_sem)
        lo, hi = _unpack_bf16_pairs(cur[...])
        x_ref[:, :HALF_D] = lo.astype(BF16)
        x_ref[:, HALF_D:] = hi.astype(BF16)
        nbase = jnp.minimum(i + 1, n_used - 1) * MOE_ROWS
        issue(nbase, 0, ISSUE_SPLIT[0], nxt, nxt_sem)

        @pl.when(n_used > i)
        def _():
            gu = _dot(x_ref[...], wgu_ref[...]) + bgu_ref[...]
            gate = jnp.minimum(gu[:, :D_EXPERT], SWIGLU_LIMIT)
            up = jnp.clip(gu[:, D_EXPERT:], -SWIGLU_LIMIT, SWIGLU_LIMIT)
            act_ref[...] = ((up + 1.0) * gate * _sigmoid(SWIGLU_ALPHA * gate)).astype(BF16)
            issue(nbase, ISSUE_SPLIT[0], ISSUE_SPLIT[1] - ISSUE_SPLIT[0], nxt, nxt_sem)

        @pl.when(n_used >= i + 1)
        def _():
            y = _dot(act_ref[...], wd_ref[...]) + bd_ref[...]
            o_ref[...] = _pack_bf16_pairs(y)
            issue(nbase, ISSUE_SPLIT[1], MOE_ROWS - ISSUE_SPLIT[1], nxt, nxt_sem)

        @pl.when(i == n_used - 1)
        def _():
            drain(nxt, nxt_sem)

    @pl.when(jnp.logical_and(i < n_used, i % 2 == 0))
    def _():
        step(xbuf0, sem.at[0], xbuf1, sem.at[1])

    @pl.when(jnp.logical_and(i < n_used, i % 2 == 1))
    def _():
        step(xbuf1, sem.at[1], xbuf0, sem.at[0])

    @pl.when(i >= n_used)
    def _():
        o_ref[...] = jnp.zeros_like(o_ref)


def _experts(h, block_e, n_used, slot_tok, w_gu, b_gu, w_down, b_down, l):
    n_slots = slot_tok.shape[0]
    d = D_MODEL
    grid_spec = pltpu.PrefetchScalarGridSpec(
        num_scalar_prefetch=3,
        grid=(n_slots // MOE_ROWS,),
        in_specs=[pl.BlockSpec(memory_space=pl.ANY),
                  pl.BlockSpec((None, None, d, 2 * D_EXPERT), lambda i, be, nu, st: (l, be[i], 0, 0)),
                  pl.BlockSpec((None, None, 1, 2 * D_EXPERT), lambda i, be, nu, st: (l, be[i], 0, 0)),
                  pl.BlockSpec((None, None, D_EXPERT, d), lambda i, be, nu, st: (l, be[i], 0, 0)),
                  pl.BlockSpec((None, None, 1, d), lambda i, be, nu, st: (l, be[i], 0, 0))],
        out_specs=pl.BlockSpec((MOE_ROWS, HALF_D), lambda i, be, nu, st: (i, 0)),
        scratch_shapes=[pltpu.VMEM((MOE_ROWS, HALF_D), jnp.uint32), pltpu.VMEM((MOE_ROWS, HALF_D), jnp.uint32),
                        pltpu.VMEM((MOE_ROWS, d), BF16), pltpu.VMEM((MOE_ROWS, D_EXPERT), BF16),
                        pltpu.SemaphoreType.DMA((2,))],
    )
    return pl.pallas_call(
        _expert_kernel,
        grid_spec=grid_spec,
        out_shape=jax.ShapeDtypeStruct((n_slots, HALF_D), jnp.uint32),
        compiler_params=_params("arbitrary"),
        name="experts",
    )(block_e, n_used, slot_tok, h, w_gu, b_gu, w_down, b_down)


COMBINE_TM = 128


def _combine_kernel(dest_ref, ys_hbm, wt_ref, x_ref, gt_ref, o_ref, buf0, buf1, sem):
    i = pl.program_id(0)
    n = pl.num_programs(0)
    tm = COMBINE_TM

    def issue(tile, buf, s):
        base = tile * (tm * TOP_K)
        for r in range(tm):
            for k in range(TOP_K):
                _row_copy(ys_hbm, dest_ref[base + r * TOP_K + k], buf.at[k], r, s).start()

    def drain(buf, s):
        for r in range(tm):
            for k in range(TOP_K):
                _row_copy(ys_hbm, 0, buf.at[k], r, s).wait()

    @pl.when(i == 0)
    def _():
        issue(0, buf0, sem.at[0])

    def prefetch(cur, cur_sem, nxt, nxt_sem):
        drain(cur, cur_sem)
        issue(jnp.minimum(i + 1, n - 1), nxt, nxt_sem)

    def compute(cur, nxt, nxt_sem):
        acc_lo = acc_hi = None
        for k in range(TOP_K):
            lo, hi = _unpack_bf16_pairs(cur[k])
            w = wt_ref[:, k:k + 1]
            acc_lo = w * lo if acc_lo is None else acc_lo + w * lo
            acc_hi = w * hi if acc_hi is None else acc_hi + w * hi
        o_ref[:, :HALF_D] = x_ref[:, :HALF_D] + gt_ref[:, :HALF_D] * acc_lo
        o_ref[:, HALF_D:] = x_ref[:, HALF_D:] + gt_ref[:, HALF_D:] * acc_hi

        @pl.when(i == n - 1)
        def _():
            drain(nxt, nxt_sem)

    even = i % 2 == 0

    @pl.when(even)
    def _():
        prefetch(buf0, sem.at[0], buf1, sem.at[1])

    @pl.when(jnp.logical_not(even))
    def _():
        prefetch(buf1, sem.at[1], buf0, sem.at[0])

    @pl.when(i % 2 < 1)
    def _():
        compute(buf0, buf1, sem.at[1])

    @pl.when(i % 2 >= 1)
    def _():
        compute(buf1, buf0, sem.at[0])


def _combine(dest, ys, wt, x, mod4, seq):
    m, d = x.shape
    tm = COMBINE_TM
    per_b = seq // tm
    grid_spec = pltpu.PrefetchScalarGridSpec(
        num_scalar_prefetch=1,
        grid=(m // tm,),
        in_specs=[pl.BlockSpec(memory_space=pl.ANY),
                  pl.BlockSpec((tm, LANES), lambda i, ds: (i, 0)),
                  pl.BlockSpec((tm, d), lambda i, ds: (i, 0)),
                  pl.BlockSpec((None, None, 1, d), lambda i, ds: (i // per_b, 5, 0, 0))],
        out_specs=pl.BlockSpec((tm, d), lambda i, ds: (i, 0)),
        scratch_shapes=[pltpu.VMEM((TOP_K, tm, HALF_D), jnp.uint32), pltpu.VMEM((TOP_K, tm, HALF_D), jnp.uint32),
                        pltpu.SemaphoreType.DMA((2,))],
    )
    return pl.pallas_call(
        _combine_kernel,
        grid_spec=grid_spec,
        out_shape=jax.ShapeDtypeStruct((m, d), F32),
        compiler_params=_params("arbitrary"),
        name="moe_combine",
    )(dest, ys, wt, x, mod4)


def _moe(x, g, l, mod4, w_router, b_router, w_gu, b_gu, w_down, b_down, seq):
    m = x.shape[0]
    h, idx, wt, rank, cnt = _router(x, g, l, mod4, w_router, b_router, seq)
    counts = cnt[0, :N_EXPERTS]
    padded = ((counts + MOE_ROWS - 1) // MOE_ROWS) * MOE_ROWS
    ends = jnp.cumsum(padded)
    starts = ends - padded
    dest = (starts[idx[:, :TOP_K]] + rank[:, :TOP_K]).reshape(-1)
    n_slots = m * TOP_K + N_EXPERTS * MOE_ROWS
    n_blocks = n_slots // MOE_ROWS
    slot_tok = jnp.zeros((n_slots,), jnp.int32).at[dest].set(jnp.arange(m * TOP_K, dtype=jnp.int32) // TOP_K)
    block_start = jnp.arange(n_blocks, dtype=jnp.int32) * MOE_ROWS
    block_e = jnp.minimum(jnp.sum(ends[None, :] <= block_start[:, None], axis=1), N_EXPERTS - 1).astype(jnp.int32)
    n_used = (ends[-1:] // MOE_ROWS).astype(jnp.int32)
    ys = _experts(h, block_e, n_used, slot_tok, w_gu, b_gu, w_down, b_down, l)
    return _combine(dest, ys, wt, x, mod4, seq)


def _pad_lanes(v):
    return jnp.pad(v, ((0, 0), (0, LANES - v.shape[-1])))[:, None, :]


def kernel(x, c, w_mod, b_mod, g_norm_mix, g_norm_ffn, w_in, hg_lower_bound, hg_onorm, s5_lambda_re, s5_lambda_im, s5_log_dt, s5_b_re, s5_b_im, s5_c_re, s5_c_im, s5_d, s5_w_glu, s5_b_glu, m2_conv_w, m2_conv_b, m2_dt_bias, m2_a_log, m2_d, m2_norm, rk_mu, rk_w0, rk_w2, rk_a0, rk_a2, rk_g2, rk_k_k, rk_k_a, rk_r_k, rk_ln_w, rk_ln_b, w_branch, w_out, w_router, b_router, w_gu, b_gu, w_down, b_down, g_final):
    batch, seq, d = x.shape
    depth = w_in.shape[0]
    n = batch * seq
    dt0 = N_MAIN
    w_tail = jnp.concatenate([w_in[:, :, dt0 + M2_HEADS:], w_in[:, :, dt0:dt0 + M2_HEADS],
                              jnp.zeros((depth, d, LANES - M2_HEADS), F32)], axis=-1)
    w_branch_b = w_branch.astype(BF16)
    w_out_b = w_out.astype(BF16)
    w_gu_b = w_gu.astype(BF16)
    w_down_b = w_down.astype(BF16)
    w_router_p = jnp.pad(w_router, ((0, 0), (0, 0), (0, LANES - N_EXPERTS)))
    b_router_p = _pad_lanes(b_router)
    b_gu4 = b_gu[:, :, None, :]
    b_down4 = b_down[:, :, None, :]
    lbs = jax.nn.softmax(hg_lower_bound.astype(F32), axis=0)
    lbs = jnp.cumsum(lbs, axis=0) - lbs[0]
    dt_bias_p = _pad_lanes(m2_dt_bias)
    a_log_p = _pad_lanes(m2_a_log)
    m2_d_rep = jnp.repeat(m2_d, M2_HEADDIM, axis=-1)[:, None, :]
    c_pad = jnp.pad(c, ((0, 8 - batch), (0, 0)))

    x2 = x.reshape(n, d)
    for l in range(depth):
        mod = _modulation(c_pad, w_mod, b_mod[:, None, :], l)
        mod4 = mod[:batch].reshape(batch, 6, 1, d)
        h = _norm_modulate(x2, g_norm_mix[:, None, :], l, mod4, 0, 1, seq, BF16)
        proj = _matmul(h, w_in, l, N_MAIN, 512, 1024, F32, "in_proj_main")
        tail = _matmul(h, w_tail, l, N_TAIL, 512, 640, F32, "in_proj_tail")
        ya = _hgrn2(proj, lbs[l][None], hg_onorm[l][None], batch, seq)
        tables = _s5_tables(s5_lambda_re[l], s5_lambda_im[l], s5_log_dt[l], s5_b_re[l], s5_b_im[l],
                            s5_c_re[l], s5_c_im[l], seq // S5_T)
        yb = _s5(proj, tables, s5_d[l][None], s5_w_glu[l].astype(BF16), s5_b_glu[l][None], batch, seq)
        yc = _mamba2(proj, tail, m2_conv_w[l], m2_conv_b[l][None], dt_bias_p[l], a_log_p[l], m2_d_rep[l],
                     m2_norm[l][None], batch, seq)
        yd = _rwkv7(tail, rk_mu[l][None], rk_w0[l][None], rk_w2[l], rk_a0[l][None], rk_a2[l], rk_g2[l],
                    rk_k_k[l][None], rk_k_a[l][None], rk_r_k[l].reshape(1, BRANCH), rk_ln_w[l][None],
                    rk_ln_b[l][None], batch, seq)
        merged = _merge([ya, yb, yc, yd], proj, w_branch_b, l)
        x2 = _out_proj_residual(merged, w_out_b, l, x2, mod4, 2, seq)
        x2 = _moe(x2, g_norm_ffn[:, None, :], l, mod4, w_router_p, b_router_p, w_gu_b, b_gu4,
                  w_down_b, b_down4, seq)
    return _final_norm(x2, g_final[None]).reshape(batch, seq, d)
```

```python
import functools
import math

import jax
import jax.numpy as jnp
from jax import lax
from jax.experimental import pallas as pl
from jax.experimental.pallas import tpu as pltpu

F32 = jnp.float32
BF16 = jnp.bfloat16
HI = lax.Precision.HIGHEST

D_MODEL = 2048
CHUNK = 64
BRANCH = 512
HG_HEADS, HG_DK = 4, 128
S5_CH, S5_GROUPS, S5_STATE = 16, 32, 64
S5_T = 8
S5_PACK = 8
M2_HEADS, M2_HEADDIM, M2_GROUPS, M2_STATE, M2_CONV, M2_XBC = 8, 64, 2, 128, 4, 1024
RK_HEADS, RK_HEADSIZE = 8, 64
RK_DECAY_LORA, RK_A_LORA, RK_GATE_LORA, RK_WIDTH = 64, 64, 128, 1792
RK_GN_EPS = 64e-5
GATE_IN = 4 * D_MODEL
N_MAIN = GATE_IN + 4 * BRANCH + BRANCH + BRANCH + M2_XBC
N_TAIL = RK_WIDTH + 128
N_EXPERTS, TOP_K, D_EXPERT = 32, 4, 768
SWIGLU_LIMIT, SWIGLU_ALPHA = 7.0, 1.702
MOE_ROWS = 256
NORM_EPS = 1e-6
LANES = 128
VMEM_LIMIT = 48 * 1024 * 1024


def _dot(a, b, prec=None):
    return jnp.dot(a, b, preferred_element_type=F32, precision=prec)


def _dot_nt(a, b, prec=None):
    return lax.dot_general(a, b, (((1,), (1,)), ((), ())), preferred_element_type=F32, precision=prec)


def _dot_tn(a, b, prec=None):
    return lax.dot_general(a, b, (((0,), (0,)), ((), ())), preferred_element_type=F32, precision=prec)


def _bdot(a, b):
    return _dot(a.astype(BF16), b.astype(BF16))


def _bdot_nt(a, b):
    return _dot_nt(a.astype(BF16), b.astype(BF16))


def _bdot_tn(a, b):
    return _dot_tn(a.astype(BF16), b.astype(BF16))


def _sigmoid(x):
    return 1.0 / (1.0 + jnp.exp(-x))


def _silu(x):
    return x * _sigmoid(x)


def _softplus(x):
    return jnp.maximum(x, 0.0) + jnp.log(1.0 + jnp.exp(-jnp.abs(x)))


def _tril(n, k=0):
    r = lax.broadcasted_iota(jnp.int32, (n, n), 0)
    c = lax.broadcasted_iota(jnp.int32, (n, n), 1)
    return (c - r) <= k


def _params(*sem):
    return pltpu.CompilerParams(dimension_semantics=sem, vmem_limit_bytes=VMEM_LIMIT)


def _mm_kernel(x_ref, w_ref, o_ref, wbf_ref):
    @pl.when(pl.program_id(1) == 0)
    def _():
        wbf_ref[...] = w_ref[...].astype(BF16)

    o_ref[...] = _dot(x_ref[...], wbf_ref[...]).astype(o_ref.dtype)


def _mod_kernel(c_ref, w_ref, b_ref, o_ref):
    o_ref[...] = _dot(_silu(c_ref[...]).astype(BF16), w_ref[...].astype(BF16)) + b_ref[...]


def _mm_resid_kernel(x_ref, w_ref, r_ref, g_ref, o_ref):
    o_ref[...] = r_ref[...] + g_ref[...] * _dot(x_ref[...], w_ref[...])


def _matmul(x, w, l, n_cols, tm, tn, out_dtype, name):
    m, k = x.shape
    return pl.pallas_call(
        _mm_kernel,
        grid=(n_cols // tn, m // tm),
        in_specs=[pl.BlockSpec((tm, k), lambda j, i: (i, 0)),
                  pl.BlockSpec((None, k, tn), lambda j, i: (l, 0, j))],
        out_specs=pl.BlockSpec((tm, tn), lambda j, i: (i, j)),
        out_shape=jax.ShapeDtypeStruct((m, n_cols), out_dtype),
        scratch_shapes=[pltpu.VMEM((k, tn), BF16)],
        compiler_params=_params("parallel", "arbitrary"),
        name=name,
    )(x, w)


def _modulation(cond, w_mod, b_mod, l):
    m, k = cond.shape
    n = w_mod.shape[-1]
    tn = 1024
    return pl.pallas_call(
        _mod_kernel,
        grid=(n // tn,),
        in_specs=[pl.BlockSpec((m, k), lambda j: (0, 0)),
                  pl.BlockSpec((None, k, tn), lambda j: (l, 0, j)),
                  pl.BlockSpec((None, 1, tn), lambda j: (l, 0, j))],
        out_specs=pl.BlockSpec((m, tn), lambda j: (0, j)),
        out_shape=jax.ShapeDtypeStruct((m, n), F32),
        compiler_params=_params("parallel"),
        name="modulation",
    )(cond, w_mod, b_mod)


def _out_proj_residual(merged, w_out, l, x, mod4, gate_idx, seq):
    m, k = merged.shape
    tm, tn = 512, 512
    per_b = seq // tm
    return pl.pallas_call(
        _mm_resid_kernel,
        grid=(D_MODEL // tn, m // tm),
        in_specs=[pl.BlockSpec((tm, k), lambda j, i: (i, 0)),
                  pl.BlockSpec((None, k, tn), lambda j, i: (l, 0, j)),
                  pl.BlockSpec((tm, tn), lambda j, i: (i, j)),
                  pl.BlockSpec((None, None, 1, tn), lambda j, i: (i // per_b, gate_idx, 0, j))],
        out_specs=pl.BlockSpec((tm, tn), lambda j, i: (i, j)),
        out_shape=jax.ShapeDtypeStruct((m, D_MODEL), F32),
        compiler_params=_params("parallel", "parallel"),
        name="out_proj_residual",
    )(merged, w_out, x, mod4)


def _rms(x, g):
    return x * lax.rsqrt(jnp.mean(x * x, axis=-1, keepdims=True) + NORM_EPS) * g


def _normmod_kernel(x_ref, g_ref, sc_ref, sh_ref, o_ref):
    y = _rms(x_ref[...], g_ref[...])
    o_ref[...] = (y * (1.0 + sc_ref[...]) + sh_ref[...]).astype(o_ref.dtype)


def _norm_modulate(x, g, l, mod4, shift_idx, scale_idx, seq, out_dtype):
    m, d = x.shape
    ts = 512
    per_b = seq // ts
    return pl.pallas_call(
        _normmod_kernel,
        grid=(m // ts,),
        in_specs=[pl.BlockSpec((ts, d), lambda i: (i, 0)),
                  pl.BlockSpec((None, 1, d), lambda i: (l, 0, 0)),
                  pl.BlockSpec((None, None, 1, d), lambda i: (i // per_b, scale_idx, 0, 0)),
                  pl.BlockSpec((None, None, 1, d), lambda i: (i // per_b, shift_idx, 0, 0))],
        out_specs=pl.BlockSpec((ts, d), lambda i: (i, 0)),
        out_shape=jax.ShapeDtypeStruct((m, d), out_dtype),
        compiler_params=_params("parallel"),
        name="norm_modulate",
    )(x, g, mod4, mod4)


def _final_norm_kernel(x_ref, g_ref, o_ref):
    o_ref[...] = _rms(x_ref[...], g_ref[...])


def _final_norm(x, g):
    m, d = x.shape
    ts = 512
    return pl.pallas_call(
        _final_norm_kernel,
        grid=(m // ts,),
        in_specs=[pl.BlockSpec((ts, d), lambda i: (i, 0)), pl.BlockSpec((1, d), lambda i: (0, 0))],
        out_specs=pl.BlockSpec((ts, d), lambda i: (i, 0)),
        out_shape=jax.ShapeDtypeStruct((m, d), F32),
        compiler_params=_params("parallel"),
        name="final_norm",
    )(x, g)


HG_SUB = 16


def _hgrn2_kernel(q_ref, f_ref, i_ref, g_ref, lb_ref, on_ref, o_ref, state_ref):
    @pl.when(pl.program_id(1) == 0)
    def _():
        state_ref[...] = jnp.zeros_like(state_ref)

    sub_mask = _tril(HG_SUB)
    lb = lb_ref[...]
    forget = lb + (1.0 - lb) * _sigmoid(f_ref[...])
    k = 1.0 - forget
    q = _silu(q_ref[...])
    v = i_ref[...]
    cum = _dot(_tril(CHUNK).astype(F32), jnp.log(forget), HI)
    last = cum[CHUNK - 1:CHUNK, :]
    q_dec = q * jnp.exp(cum)
    k_end = k * jnp.exp(last - cum)
    heads = range(HG_HEADS)
    sls = [slice(h * HG_DK, (h + 1) * HG_DK) for h in heads]
    sts = [state_ref[h] for h in heads]
    o_inter = [_bdot_nt(q_dec[:, sls[h]], sts[h]) for h in heads]
    for h in heads:
        state_ref[h] = sts[h] * jnp.exp(last[:, sls[h]]) + _bdot_tn(v[:, sls[h]], k_end[:, sls[h]])
    rows = [[None] * (CHUNK // HG_SUB) for _ in heads]
    for i in range(CHUNK // HG_SUB):
        r0 = i * HG_SUB
        if i > 0:
            ci = cum[r0 - 1:r0, :]
            qp = q[r0:r0 + HG_SUB] * jnp.exp(cum[r0:r0 + HG_SUB] - ci)
            kp = k[:r0] * jnp.exp(ci - cum[:r0])
            sc = [_bdot_nt(qp[:, sls[h]], kp[:, sls[h]]) for h in heads]
            off = [_bdot(sc[h], v[:r0, sls[h]]) for h in heads]
        for h in heads:
            sl = sls[h]
            qi = q[r0:r0 + HG_SUB, sl]
            cumi = cum[r0:r0 + HG_SUB, sl]
            oi = o_inter[h][r0:r0 + HG_SUB]
            if i > 0:
                oi = oi + off[h]
            for s in range(HG_SUB):
                diff = jnp.where(sub_mask[:, s:s + 1], cumi - cumi[s:s + 1, :], -jnp.inf)
                w = jnp.sum(qi * jnp.exp(diff) * k[r0 + s:r0 + s + 1, sl], axis=-1, keepdims=True)
                oi = oi + w * v[r0 + s:r0 + s + 1, sl]
            rows[h][i] = oi
    outs = [_rms(jnp.concatenate(rows[h], axis=0), on_ref[...]) for h in heads]
    o_ref[...] = jnp.concatenate(outs, axis=-1) * _silu(g_ref[...])


def _hgrn2(proj, lb, onorm, batch, seq):
    nc = seq // CHUNK
    col0 = GATE_IN // BRANCH
    spec = lambda off: pl.BlockSpec((CHUNK, BRANCH), lambda b, c: (b * nc + c, col0 + off))
    vec = lambda n: pl.BlockSpec((1, n), lambda b, c: (0, 0))
    return pl.pallas_call(
        _hgrn2_kernel,
        grid=(batch, nc),
        in_specs=[spec(0), spec(1), spec(2), spec(3), vec(BRANCH), vec(HG_DK)],
        out_specs=pl.BlockSpec((CHUNK, BRANCH), lambda b, c: (b * nc + c, 0)),
        out_shape=jax.ShapeDtypeStruct((batch * seq, BRANCH), F32),
        scratch_shapes=[pltpu.VMEM((HG_HEADS, HG_DK, HG_DK), F32)],
        compiler_params=_params("parallel", "arbitrary"),
        name="hgrn2",
    )(proj, proj, proj, proj, lb, onorm)


def _mamba2_kernel(z_ref, xbc_ref, dt_ref, cw_ref, cb_ref, dtb_ref, alog_ref, d_ref, ng_ref,
                   o_ref, ext_ref, state_ref):
    @pl.when(pl.program_id(1) == 0)
    def _():
        ext_ref[0:8, :] = jnp.zeros((8, M2_XBC), F32)
        state_ref[...] = jnp.zeros_like(state_ref)

    ext_ref[8:8 + CHUNK, :] = xbc_ref[...]
    conv = cb_ref[...]
    for j in range(M2_CONV):
        conv = conv + cw_ref[j:j + 1, :] * ext_ref[8 - (M2_CONV - 1) + j:8 - (M2_CONV - 1) + j + CHUNK, :]
    ext_ref[0:8, :] = xbc_ref[CHUNK - 8:CHUNK, :]
    xa = _silu(conv)
    gn = M2_GROUPS * M2_STATE
    xs = xa[:, :BRANCH]
    dt = _softplus(dt_ref[...] + dtb_ref[...])
    adt = -jnp.exp(alog_ref[...]) * dt
    tri = _tril(CHUNK)
    cum = _dot(tri.astype(F32), adt, HI)
    cum_t = cum.T
    cb = []
    for g in range(M2_GROUPS):
        bm = xa[:, BRANCH + g * M2_STATE:BRANCH + (g + 1) * M2_STATE]
        cm = xa[:, BRANCH + gn + g * M2_STATE:BRANCH + gn + (g + 1) * M2_STATE]
        cb.append((bm, cm, _bdot_nt(cm, bm)))
    ys = []
    for h in range(M2_HEADS):
        bm, cm, cbg = cb[h // (M2_HEADS // M2_GROUPS)]
        cum_h = cum[:, h:h + 1]
        lmat = jnp.exp(jnp.where(tri, cum_h - cum_t[h:h + 1, :], -jnp.inf))
        x_h = xs[:, h * M2_HEADDIM:(h + 1) * M2_HEADDIM]
        xdt = x_h * dt[:, h:h + 1]
        st = state_ref[h]
        y = _bdot(cbg * lmat, xdt) + jnp.exp(cum_h) * _bdot(cm, st)
        last = cum[CHUNK - 1:CHUNK, h:h + 1]
        state_ref[h] = jnp.exp(last) * st + _bdot_tn(bm * jnp.exp(last - cum_h), xdt)
        ys.append(y)
    y = jnp.concatenate(ys, axis=-1) + d_ref[...] * xs
    y = y * _silu(z_ref[...])
    gw = BRANCH // M2_GROUPS
    outs = [_rms(y[:, g * gw:(g + 1) * gw], ng_ref[:, g * gw:(g + 1) * gw]) for g in range(M2_GROUPS)]
    o_ref[...] = jnp.concatenate(outs, axis=-1)


def _mamba2(proj, tail, conv_w, conv_b, dt_bias, a_log, d_rep, norm_g, batch, seq):
    nc = seq // CHUNK
    zcol = (GATE_IN + 5 * BRANCH) // BRANCH
    xcol = (GATE_IN + 6 * BRANCH) // M2_XBC
    vec = lambda r, n: pl.BlockSpec((r, n), lambda b, c: (0, 0))
    return pl.pallas_call(
        _mamba2_kernel,
        grid=(batch, nc),
        in_specs=[pl.BlockSpec((CHUNK, BRANCH), lambda b, c: (b * nc + c, zcol)),
                  pl.BlockSpec((CHUNK, M2_XBC), lambda b, c: (b * nc + c, xcol)),
                  pl.BlockSpec((CHUNK, LANES), lambda b, c: (b * nc + c, RK_WIDTH // LANES)),
                  vec(M2_CONV, M2_XBC), vec(1, M2_XBC), vec(1, LANES), vec(1, LANES),
                  vec(1, BRANCH), vec(1, BRANCH)],
        out_specs=pl.BlockSpec((CHUNK, BRANCH), lambda b, c: (b * nc + c, 0)),
        out_shape=jax.ShapeDtypeStruct((batch * seq, BRANCH), F32),
        scratch_shapes=[pltpu.VMEM((8 + CHUNK, M2_XBC), F32),
                        pltpu.VMEM((M2_HEADS, M2_STATE, M2_HEADDIM), F32)],
        compiler_params=_params("parallel", "arbitrary"),
        name="mamba2",
    )(proj, proj, tail, conv_w, conv_b, dt_bias, a_log, d_rep, norm_g)


def _unit_lower_inverses(mats):
    n = mats[0].shape[0]
    eye = (lax.broadcasted_iota(jnp.int32, (n, n), 0) == lax.broadcasted_iota(jnp.int32, (n, n), 1)).astype(F32)
    invs = [eye - a for a in mats]
    pows = list(mats)
    k = 2
    while k < n:
        pows = [_bdot(p, p) for p in pows]
        invs = [_bdot(inv, eye + p) for inv, p in zip(invs, pows)]
        k *= 2
    return invs


def _rwkv7_kernel(p_ref, mu_ref, w0_ref, w2_ref, a0_ref, a2_ref, g2_ref, kk_ref, ka_ref, rk_ref,
                  lnw_ref, lnb_ref, o_ref, ext_ref, state_ref):
    @pl.when(pl.program_id(1) == 0)
    def _():
        ext_ref[0:8, :] = jnp.zeros((8, RK_WIDTH), F32)
        state_ref[...] = jnp.zeros_like(state_ref)

    p = p_ref[...]
    ext_ref[8:8 + CHUNK, :] = p
    prev = ext_ref[7:7 + CHUNK, :]
    ext_ref[0:8, :] = p_ref[CHUNK - 8:CHUNK, :]
    pm = p + (prev - p) * mu_ref[...]
    b3 = 3 * BRANCH
    r = pm[:, :BRANCH]
    k = pm[:, BRANCH:2 * BRANCH]
    v = pm[:, 2 * BRANCH:b3]
    w_lo = pm[:, b3:b3 + RK_DECAY_LORA]
    a_lo = pm[:, b3 + RK_DECAY_LORA:b3 + RK_DECAY_LORA + RK_A_LORA]
    g_lo = pm[:, b3 + RK_DECAY_LORA + RK_A_LORA:]
    w = -_softplus(-(w0_ref[...] + _bdot(jnp.tanh(w_lo), w2_ref[...]))) - 0.5
    logw = -jnp.exp(w)
    a = _sigmoid(a0_ref[...] + _bdot(a_lo, a2_ref[...]))
    g = _bdot(_sigmoid(g_lo), g2_ref[...])
    kk = k * kk_ref[...]
    k = k * (1.0 + (a - 1.0) * ka_ref[...])
    c, hs = CHUNK, RK_HEADSIZE
    row = lax.broadcasted_iota(jnp.int32, (2 * c, 2 * c), 0)
    col = lax.broadcasted_iota(jnp.int32, (2 * c, 2 * c), 1)
    quad_mask = (col & (c - 1)) < (row & (c - 1)) + (row >= c).astype(jnp.int32)
    cum = _dot(_tril(c).astype(F32), logw, HI)
    last = cum[c - 1:c, :]
    gam = jnp.exp(cum)
    ginv = jnp.exp(-cum)
    to_end = jnp.exp(last - cum)
    gam_prev = jnp.exp(cum - logw)
    heads = range(RK_HEADS)
    sls = [slice(h * hs, (h + 1) * hs) for h in heads]
    kap = []
    for sl in sls:
        kh = kk[:, sl]
        kap.append(kh / jnp.maximum(jnp.sqrt(jnp.sum(kh * kh, axis=-1, keepdims=True)), 1e-12))
    beta = [kap[h] * a[:, sls[h]] for h in heads]
    lhs = [jnp.concatenate([kap[h] * gam_prev[:, sls[h]], r[:, sls[h]] * gam[:, sls[h]]], axis=0)
           for h in heads]
    rhs = [jnp.concatenate([beta[h] * ginv[:, sls[h]], k[:, sls[h]] * ginv[:, sls[h]]], axis=0)
           for h in heads]
    quad = [jnp.where(quad_mask, _bdot_nt(lhs[h], rhs[h]), 0.0) for h in heads]
    invs = _unit_lower_inverses([q[:c, :c] for q in quad])
    vs = [v[:, sl] for sl in sls]
    akv = [_bdot(quad[h][:, c:], vs[h]) for h in heads]
    sts = [state_ref[h] for h in heads]
    from_state = [_bdot_nt(lhs[h], sts[h]) for h in heads]
    us = [_bdot(invs[h], -(from_state[h][:c] + akv[h][:c])) for h in heads]
    ys = [from_state[h][c:] + akv[h][c:] + _bdot(quad[h][c:, :c], us[h]) for h in heads]
    for h in heads:
        sl = sls[h]
        upd = _bdot_tn(jnp.concatenate([us[h], vs[h]], axis=0),
                       jnp.concatenate([beta[h] * to_end[:, sl], k[:, sl] * to_end[:, sl]], axis=0))
        state_ref[h] = sts[h] * gam[c - 1:c, sl] + upd
    outs = []
    for h in heads:
        sl = sls[h]
        mean = jnp.mean(ys[h], axis=-1, keepdims=True)
        yc = ys[h] - mean
        var = jnp.mean(yc * yc, axis=-1, keepdims=True)
        yn = yc * lax.rsqrt(var + RK_GN_EPS) * lnw_ref[:, sl] + lnb_ref[:, sl]
        bonus = jnp.sum(r[:, sl] * k[:, sl] * rk_ref[:, sl], axis=-1, keepdims=True) * vs[h]
        outs.append(yn + bonus)
    o_ref[...] = jnp.concatenate(outs, axis=-1) * g


def _rwkv7(tail, mu, w0, w2, a0, a2, g2, k_k, k_a, r_k, ln_w, ln_b, batch, seq):
    nc = seq // CHUNK
    vec = lambda r, n: pl.BlockSpec((r, n), lambda b, c: (0, 0))
    return pl.pallas_call(
        _rwkv7_kernel,
        grid=(batch, nc),
        in_specs=[pl.BlockSpec((CHUNK, RK_WIDTH), lambda b, c: (b * nc + c, 0)),
                  vec(1, RK_WIDTH), vec(1, BRANCH), vec(RK_DECAY_LORA, BRANCH), vec(1, BRANCH),
                  vec(RK_A_LORA, BRANCH), vec(RK_GATE_LORA, BRANCH), vec(1, BRANCH), vec(1, BRANCH),
                  vec(1, BRANCH), vec(1, BRANCH), vec(1, BRANCH)],
        out_specs=pl.BlockSpec((CHUNK, BRANCH), lambda b, c: (b * nc + c, 0)),
        out_shape=jax.ShapeDtypeStruct((batch * seq, BRANCH), F32),
        scratch_shapes=[pltpu.VMEM((8 + CHUNK, RK_WIDTH), F32),
                        pltpu.VMEM((RK_HEADS, RK_HEADSIZE, RK_HEADSIZE), F32)],
        compiler_params=_params("parallel", "arbitrary"),
        name="rwkv7",
    )(tail, mu, w0, w2, a0, a2, g2, k_k, k_a, r_k, ln_w, ln_b)


def _s5_tables(lam_re, lam_im, log_dt, b_re, b_im, c_re, c_im, n_chunks):
    t = S5_T
    g, p, i = b_re.shape
    dt = jnp.exp(log_dt)[:, None]

    def apow(n):
        e = n.astype(F32)[..., None, None]
        mag = jnp.exp(e * lam_re * dt)
        return mag * jnp.cos(e * lam_im * dt), mag * jnp.sin(e * lam_im * dt)

    ab_re, ab_im = apow(jnp.ones((), F32))
    den = lam_re * lam_re + lam_im * lam_im
    nr = ab_re - 1.0
    coef_re = ((nr * lam_re + ab_im * lam_im) / den)[..., None]
    coef_im = ((ab_im * lam_re - nr * lam_im) / den)[..., None]
    bt_re = coef_re * b_re - coef_im * b_im
    bt_im = coef_re * b_im + coef_im * b_re
    ar, ai = apow(jnp.arange(t + 1))
    cr, ci = c_re[None], c_im[None]
    ca_re = cr * ar[:, :, None, :] - ci * ai[:, :, None, :]
    ca_im = cr * ai[:, :, None, :] + ci * ar[:, :, None, :]
    kern = (jnp.einsum('tgop,gpi->tgoi', ca_re[:t], bt_re, precision=HI)
            - jnp.einsum('tgop,gpi->tgoi', ca_im[:t], bt_im, precision=HI))
    lag = jnp.arange(t)[None, :] - jnp.arange(t)[:, None]
    toep = jnp.where((lag >= 0)[:, :, None, None, None], kern[jnp.maximum(lag, 0)], 0.0)
    toep = toep.transpose(2, 0, 4, 1, 3).reshape(g, t * i, t * i)
    br, bi = ar[:t][::-1], ai[:t][::-1]
    bs_re = br[..., None] * bt_re - bi[..., None] * bt_im
    bs_im = br[..., None] * bt_im + bi[..., None] * bt_re
    bst = jnp.concatenate([bs_re, bs_im], axis=2).transpose(1, 0, 3, 2).reshape(g, t * i, 2 * p)
    cst = jnp.concatenate([ca_re[1:], -ca_im[1:]], axis=3)
    cst = cst.transpose(1, 3, 0, 2).reshape(g, 2 * p, t * i)
    levels = max(1, int(math.log2(n_chunks)))
    pr, pi = apow(t * (2 ** jnp.arange(levels)))
    m1 = jnp.concatenate([pr, pr], axis=-1)
    m2 = jnp.concatenate([-pi, pi], axis=-1)
    k = S5_PACK
    a = g // k
    eye = jnp.eye(k, dtype=F32)
    w = k * i
    toep_p = jnp.einsum('agsizo,gh->asgizho', toep.reshape(a, k, t, i, t, i), eye).reshape(a, t * w, t * w)
    bst_p = jnp.einsum('agsip,gh->asgihp', bst.reshape(a, k, t, i, 2 * p), eye).reshape(a, t * w, k * 2 * p)
    cst_p = jnp.einsum('agpzo,gh->agpzho', cst.reshape(a, k, 2 * p, t, i), eye).reshape(a, k * 2 * p, t * w)
    m1 = m1.reshape(levels, a, 1, k * 2 * p)
    m2 = m2.reshape(levels, a, 1, k * 2 * p)
    return toep_p.astype(BF16), bst_p.astype(BF16), cst_p.astype(BF16), m1, m2


def _s5_kernel(n_chunks, u_ref, toep_ref, bst_ref, cst_ref, m1_ref, m2_ref, y_ref, u_scr):
    for s in range(S5_T):
        u_scr[:, s * LANES:(s + 1) * LANES] = u_ref[:, s, :].astype(BF16)
    u = u_scr[...]
    y = _dot(u, toep_ref[...])
    x = _dot(u, bst_ref[...])
    rows = lax.broadcasted_iota(jnp.int32, x.shape, 0) & (n_chunks - 1)
    lane = lax.broadcasted_iota(jnp.int32, x.shape, 1)
    re_half = (lane & (2 * S5_STATE - 1)) < S5_STATE
    width = x.shape[1]

    def shifted(val, d):
        return jnp.where(rows >= d, pltpu.roll(val, d, axis=0), 0.0)

    def swap_re_im(val):
        return jnp.where(re_half, pltpu.roll(val, width - S5_STATE, axis=1), pltpu.roll(val, S5_STATE, axis=1))

    d = 1
    lvl = 0
    while d < n_chunks:
        xs = shifted(x, d)
        x = x + xs * m1_ref[lvl] + swap_re_im(xs) * m2_ref[lvl]
        d *= 2
        lvl += 1
    y = y + _dot(shifted(x, 1).astype(BF16), cst_ref[...])
    for s in range(S5_T):
        y_ref[:, s, :] = y[:, s * LANES:(s + 1) * LANES]


def _s5_scan(proj3, toep, bst, cst, m1, m2, l, n_chunks):
    r = proj3.shape[0]
    a, w = toep.shape[1], toep.shape[2]
    levels = m1.shape[1]
    ucol = (GATE_IN + 4 * BRANCH) // LANES
    tab = lambda: pl.BlockSpec((None, None, w, w), lambda i: (l, i, 0, 0))
    vec = lambda: pl.BlockSpec((None, levels, None, 1, w), lambda i: (l, 0, i, 0, 0))
    return pl.pallas_call(
        functools.partial(_s5_kernel, n_chunks),
        grid=(a,),
        in_specs=[pl.BlockSpec((r, S5_T, LANES), lambda i: (0, 0, ucol + i)), tab(), tab(), tab(), vec(), vec()],
        out_specs=pl.BlockSpec((r, S5_T, LANES), lambda i: (0, 0, i)),
        out_shape=jax.ShapeDtypeStruct((r, S5_T, BRANCH), F32),
        scratch_shapes=[pltpu.VMEM((r, w), BF16)],
        compiler_params=_params("parallel"),
        name="s5_scan",
    )(proj3, toep, bst, cst, m1, m2)


def _s5_post_kernel(y_ref, u_ref, d_ref, w_ref, b_ref, o_ref):
    y = y_ref[...] + d_ref[...] * u_ref[...]
    y = 0.5 * y * (1.0 + jnp.tanh(math.sqrt(2.0 / math.pi) * (y + 0.044715 * (y * y * y))))
    o_ref[...] = y * _sigmoid(_bdot(y, w_ref[...]) + b_ref[...])


def _s5_post(y_ssm, proj, d_skip, w_glu, b_glu):
    m = y_ssm.shape[0]
    tm = min(512, m)
    ucol = (GATE_IN + 4 * BRANCH) // BRANCH
    return pl.pallas_call(
        _s5_post_kernel,
        grid=(m // tm,),
        in_specs=[pl.BlockSpec((tm, BRANCH), lambda i: (i, 0)),
                  pl.BlockSpec((tm, BRANCH), lambda i: (i, ucol)),
                  pl.BlockSpec((1, BRANCH), lambda i: (0, 0)),
                  pl.BlockSpec((BRANCH, BRANCH), lambda i: (0, 0)),
                  pl.BlockSpec((1, BRANCH), lambda i: (0, 0))],
        out_specs=pl.BlockSpec((tm, BRANCH), lambda i: (i, 0)),
        out_shape=jax.ShapeDtypeStruct((m, BRANCH), F32),
        compiler_params=_params("parallel"),
        name="s5_post",
    )(y_ssm, proj, d_skip, w_glu, b_glu)


def _s5(proj, tables, l, d_skip, w_glu, b_glu, batch, seq):
    n = batch * seq
    y3 = _s5_scan(proj.reshape(n // S5_T, S5_T, proj.shape[1]), *tables, l, seq // S5_T)
    return _s5_post(y3.reshape(n, BRANCH), proj, d_skip, w_glu, b_glu)


def _merge_kernel(ya_ref, yb_ref, yc_ref, yd_ref, ga_ref, gb_ref, gc_ref, gd_ref,
                  wa_ref, wb_ref, wc_ref, wd_ref, o_ref):
    acc = None
    for y_ref, g_ref, w_ref in ((ya_ref, ga_ref, wa_ref), (yb_ref, gb_ref, wb_ref),
                                (yc_ref, gc_ref, wc_ref), (yd_ref, gd_ref, wd_ref)):
        term = _sigmoid(g_ref[...]) * _dot(y_ref[...].astype(BF16), w_ref[...])
        acc = term if acc is None else acc + term
    o_ref[...] = acc.astype(o_ref.dtype)


def _merge(ys, proj, w_branch, l):
    m = ys[0].shape[0]
    tm = 256
    yspec = pl.BlockSpec((tm, BRANCH), lambda i: (i, 0))
    gspec = lambda n: pl.BlockSpec((tm, D_MODEL), lambda i: (i, n))
    wspec = lambda n: pl.BlockSpec((None, None, BRANCH, D_MODEL), lambda i: (l, n, 0, 0))
    return pl.pallas_call(
        _merge_kernel,
        grid=(m // tm,),
        in_specs=[yspec] * 4 + [gspec(n) for n in range(4)] + [wspec(n) for n in range(4)],
        out_specs=pl.BlockSpec((tm, D_MODEL), lambda i: (i, 0)),
        out_shape=jax.ShapeDtypeStruct((m, D_MODEL), BF16),
        compiler_params=_params("parallel"),
        name="branch_merge",
    )(*ys, proj, proj, proj, proj, w_branch, w_branch, w_branch, w_branch)


ROUTER_TM = 256
HALF_D = D_MODEL // 2


def _pack_bf16_halves(lo, hi):
    lo_bits = pltpu.bitcast(lo.astype(BF16).astype(F32), jnp.uint32)
    hi_bits = pltpu.bitcast(hi.astype(BF16).astype(F32), jnp.uint32)
    return (hi_bits & jnp.uint32(0xFFFF0000)) | (lo_bits >> 16)


def _pack_bf16_pairs(v):
    return _pack_bf16_halves(v[:, :HALF_D], v[:, HALF_D:])


def _unpack_bf16_pairs(p):
    lo = pltpu.bitcast(p << 16, F32)
    hi = pltpu.bitcast(p & jnp.uint32(0xFFFF0000), F32)
    return lo, hi


def _router_kernel(x_ref, g_ref, sc_ref, sh_ref, wr_ref, br_ref,
                   h_ref, idx_ref, wt_ref, rank_ref, cnt_ref, carry_ref):
    @pl.when(pl.program_id(0) == 0)
    def _():
        carry_ref[...] = jnp.zeros_like(carry_ref)

    h = _rms(x_ref[...], g_ref[...]) * (1.0 + sc_ref[...]) + sh_ref[...]
    h_ref[...] = _pack_bf16_pairs(h)
    tm = h.shape[0]
    lane = lax.broadcasted_iota(jnp.int32, (tm, LANES), 1)
    logits = _dot(h, wr_ref[...], HI) + br_ref[...]
    masked = jnp.where(lane < N_EXPERTS, logits, -jnp.inf)
    vals, hots, idxs = [], [], []
    for _ in range(TOP_K):
        m = jnp.max(masked, axis=-1, keepdims=True)
        idx = jnp.min(jnp.where(masked == m, lane, LANES), axis=-1, keepdims=True)
        hot = lane == idx
        masked = jnp.where(hot, -jnp.inf, masked)
        vals.append(m)
        hots.append(hot)
        idxs.append(idx)
    exps = [jnp.exp(v - vals[0]) for v in vals]
    denom = exps[0] + exps[1] + exps[2] + exps[3]
    multi = sum(hot.astype(F32) for hot in hots)
    before = _dot(_tril(tm, -1).astype(BF16), multi.astype(BF16)) + carry_ref[...]
    carry_ref[...] = carry_ref[...] + jnp.sum(multi, axis=0, keepdims=True)
    idx_out = jnp.zeros((tm, LANES), jnp.int32)
    wt_out = jnp.zeros((tm, LANES), F32)
    rank_out = jnp.zeros((tm, LANES), F32)
    for k in range(TOP_K):
        rank = jnp.sum(jnp.where(hots[k], before, 0.0), axis=-1, keepdims=True)
        idx_out = jnp.where(lane == k, idxs[k], idx_out)
        wt_out = jnp.where(lane == k, exps[k] / denom, wt_out)
        rank_out = jnp.where(lane == k, rank, rank_out)
    idx_ref[...] = idx_out
    wt_ref[...] = wt_out
    rank_ref[...] = rank_out.astype(jnp.int32)
    cnt_ref[...] = carry_ref[...].astype(jnp.int32)


def _router(x, g, l, mod4, w_router, b_router, seq):
    m, d = x.shape
    tm = min(ROUTER_TM, m)
    per_b = seq // tm
    tok = lambda dt: jax.ShapeDtypeStruct((m, LANES), dt)
    tspec = pl.BlockSpec((tm, LANES), lambda i: (i, 0))
    return pl.pallas_call(
        _router_kernel,
        grid=(m // tm,),
        in_specs=[pl.BlockSpec((tm, d), lambda i: (i, 0)),
                  pl.BlockSpec((None, 1, d), lambda i: (l, 0, 0)),
                  pl.BlockSpec((None, None, 1, d), lambda i: (i // per_b, 4, 0, 0)),
                  pl.BlockSpec((None, None, 1, d), lambda i: (i // per_b, 3, 0, 0)),
                  pl.BlockSpec((None, d, LANES), lambda i: (l, 0, 0)),
                  pl.BlockSpec((None, 1, LANES), lambda i: (l, 0, 0))],
        out_specs=[pl.BlockSpec((tm, HALF_D), lambda i: (i, 0)), tspec, tspec, tspec,
                   pl.BlockSpec((1, LANES), lambda i: (0, 0))],
        out_shape=[jax.ShapeDtypeStruct((m, HALF_D), jnp.uint32), tok(jnp.int32), tok(F32), tok(jnp.int32),
                   jax.ShapeDtypeStruct((1, LANES), jnp.int32)],
        scratch_shapes=[pltpu.VMEM((1, LANES), F32)],
        compiler_params=_params("arbitrary"),
        name="router",
    )(x, g, mod4, mod4, w_router, b_router)


def _row_copy(src_hbm, row, buf, r, sem):
    return pltpu.make_async_copy(src_hbm.at[pl.ds(row, 1), :], buf.at[pl.ds(r, 1), :], sem)


ISSUE_SPLIT = (64, 160)


def _expert_kernel(be_ref, nu_ref, st_ref, h_hbm, wgu_ref, bgu_ref, wd_ref, bd_ref, o_ref,
                   xbuf0, xbuf1, x_ref, act_ref, sem):
    i = pl.program_id(0)
    n_used = nu_ref[0]

    def issue(base, r0, count, buf, s):
        for r in range(count):
            _row_copy(h_hbm, st_ref[base + r0 + r], buf, r0 + r, s).start()

    def drain(buf, s):
        for r in range(MOE_ROWS):
            _row_copy(h_hbm, 0, buf, r, s).wait()

    @pl.when(jnp.logical_and(i == 0, n_used > 0))
    def _():
        issue(0, 0, MOE_ROWS, xbuf0, sem.at[0])

    def step(cur, cur_sem, nxt, nxt_sem):
        drain(cur, cur_sem)
        lo, hi = _unpack_bf16_pairs(cur[...])
        x_ref[:, :HALF_D] = lo.astype(BF16)
        x_ref[:, HALF_D:] = hi.astype(BF16)
        nbase = jnp.minimum(i + 1, n_used - 1) * MOE_ROWS
        issue(nbase, 0, ISSUE_SPLIT[0], nxt, nxt_sem)

        @pl.when(n_used > i)
        def _():
            gu = _dot(x_ref[...], wgu_ref[...]) + bgu_ref[...]
            gate = jnp.minimum(gu[:, :D_EXPERT], SWIGLU_LIMIT)
            up = jnp.clip(gu[:, D_EXPERT:], -SWIGLU_LIMIT, SWIGLU_LIMIT)
            act_ref[...] = ((up + 1.0) * gate * _sigmoid(SWIGLU_ALPHA * gate)).astype(BF16)
            issue(nbase, ISSUE_SPLIT[0], ISSUE_SPLIT[1] - ISSUE_SPLIT[0], nxt, nxt_sem)

        @pl.when(n_used >= i + 1)
        def _():
            y = _dot(act_ref[...], wd_ref[...]) + bd_ref[...]
            o_ref[...] = _pack_bf16_pairs(y)
            issue(nbase, ISSUE_SPLIT[1], MOE_ROWS - ISSUE_SPLIT[1], nxt, nxt_sem)

        @pl.when(i == n_used - 1)
        def _():
            drain(nxt, nxt_sem)

    @pl.when(jnp.logical_and(i < n_used, i % 2 == 0))
    def _():
        step(xbuf0, sem.at[0], xbuf1, sem.at[1])

    @pl.when(jnp.logical_and(i < n_used, i % 2 == 1))
    def _():
        step(xbuf1, sem.at[1], xbuf0, sem.at[0])

    @pl.when(i >= n_used)
    def _():
        o_ref[...] = jnp.zeros_like(o_ref)


def _experts(h, block_e, n_used, slot_tok, w_gu, b_gu, w_down, b_down, l):
    n_slots = slot_tok.shape[0]
    d = D_MODEL
    grid_spec = pltpu.PrefetchScalarGridSpec(
        num_scalar_prefetch=3,
        grid=(n_slots // MOE_ROWS,),
        in_specs=[pl.BlockSpec(memory_space=pl.ANY),
                  pl.BlockSpec((None, None, d, 2 * D_EXPERT), lambda i, be, nu, st: (l, be[i], 0, 0)),
                  pl.BlockSpec((None, None, 1, 2 * D_EXPERT), lambda i, be, nu, st: (l, be[i], 0, 0)),
                  pl.BlockSpec((None, None, D_EXPERT, d), lambda i, be, nu, st: (l, be[i], 0, 0)),
                  pl.BlockSpec((None, None, 1, d), lambda i, be, nu, st: (l, be[i], 0, 0))],
        out_specs=pl.BlockSpec((MOE_ROWS, HALF_D), lambda i, be, nu, st: (i, 0)),
        scratch_shapes=[pltpu.VMEM((MOE_ROWS, HALF_D), jnp.uint32), pltpu.VMEM((MOE_ROWS, HALF_D), jnp.uint32),
                        pltpu.VMEM((MOE_ROWS, d), BF16), pltpu.VMEM((MOE_ROWS, D_EXPERT), BF16),
                        pltpu.SemaphoreType.DMA((2,))],
    )
    return pl.pallas_call(
        _expert_kernel,
        grid_spec=grid_spec,
        out_shape=jax.ShapeDtypeStruct((n_slots, HALF_D), jnp.uint32),
        compiler_params=_params("arbitrary"),
        name="experts",
    )(block_e, n_used, slot_tok, h, w_gu, b_gu, w_down, b_down)


COMBINE_TM = 128


def _combine_kernel(dest_ref, ys_hbm, wt_ref, x_ref, gt_ref, o_ref, buf0, buf1, sem):
    i = pl.program_id(0)
    n = pl.num_programs(0)
    tm = COMBINE_TM

    def issue(tile, buf, s):
        base = tile * (tm * TOP_K)
        for r in range(tm):
            for k in range(TOP_K):
                _row_copy(ys_hbm, dest_ref[base + r * TOP_K + k], buf.at[k], r, s).start()

    def drain(buf, s):
        for r in range(tm):
            for k in range(TOP_K):
                _row_copy(ys_hbm, 0, buf.at[k], r, s).wait()

    @pl.when(i == 0)
    def _():
        issue(0, buf0, sem.at[0])

    def prefetch(cur, cur_sem, nxt, nxt_sem):
        drain(cur, cur_sem)
        issue(jnp.minimum(i + 1, n - 1), nxt, nxt_sem)

    def compute(cur, nxt, nxt_sem):
        acc_lo = acc_hi = None
        for k in range(TOP_K):
            lo, hi = _unpack_bf16_pairs(cur[k])
            w = wt_ref[:, k:k + 1]
            acc_lo = w * lo if acc_lo is None else acc_lo + w * lo
            acc_hi = w * hi if acc_hi is None else acc_hi + w * hi
        o_ref[:, :HALF_D] = x_ref[:, :HALF_D] + gt_ref[:, :HALF_D] * acc_lo
        o_ref[:, HALF_D:] = x_ref[:, HALF_D:] + gt_ref[:, HALF_D:] * acc_hi

        @pl.when(i == n - 1)
        def _():
            drain(nxt, nxt_sem)

    even = i % 2 == 0

    @pl.when(even)
    def _():
        prefetch(buf0, sem.at[0], buf1, sem.at[1])

    @pl.when(jnp.logical_not(even))
    def _():
        prefetch(buf1, sem.at[1], buf0, sem.at[0])

    @pl.when(i % 2 < 1)
    def _():
        compute(buf0, buf1, sem.at[1])

    @pl.when(i % 2 >= 1)
    def _():
        compute(buf1, buf0, sem.at[0])


def _combine(dest, ys, wt, x, mod4, seq):
    m, d = x.shape
    tm = COMBINE_TM
    per_b = seq // tm
    grid_spec = pltpu.PrefetchScalarGridSpec(
        num_scalar_prefetch=1,
        grid=(m // tm,),
        in_specs=[pl.BlockSpec(memory_space=pl.ANY),
                  pl.BlockSpec((tm, LANES), lambda i, ds: (i, 0)),
                  pl.BlockSpec((tm, d), lambda i, ds: (i, 0)),
                  pl.BlockSpec((None, None, 1, d), lambda i, ds: (i // per_b, 5, 0, 0))],
        out_specs=pl.BlockSpec((tm, d), lambda i, ds: (i, 0)),
        scratch_shapes=[pltpu.VMEM((TOP_K, tm, HALF_D), jnp.uint32), pltpu.VMEM((TOP_K, tm, HALF_D), jnp.uint32),
                        pltpu.SemaphoreType.DMA((2,))],
    )
    return pl.pallas_call(
        _combine_kernel,
        grid_spec=grid_spec,
        out_shape=jax.ShapeDtypeStruct((m, d), F32),
        compiler_params=_params("arbitrary"),
        name="moe_combine",
    )(dest, ys, wt, x, mod4)


def _moe(x, g, l, mod4, w_router, b_router, w_gu, b_gu, w_down, b_down, seq):
    m = x.shape[0]
    h, idx, wt, rank, cnt = _router(x, g, l, mod4, w_router, b_router, seq)
    counts = cnt[0, :N_EXPERTS]
    padded = ((counts + MOE_ROWS - 1) // MOE_ROWS) * MOE_ROWS
    ends = jnp.cumsum(padded)
    starts = ends - padded
    dest = (starts[idx[:, :TOP_K]] + rank[:, :TOP_K]).reshape(-1)
    n_slots = m * TOP_K + N_EXPERTS * MOE_ROWS
    n_blocks = n_slots // MOE_ROWS
    slot_tok = jnp.zeros((n_slots,), jnp.int32).at[dest].set(jnp.arange(m * TOP_K, dtype=jnp.int32) // TOP_K)
    block_start = jnp.arange(n_blocks, dtype=jnp.int32) * MOE_ROWS
    block_e = jnp.minimum(jnp.sum(ends[None, :] <= block_start[:, None], axis=1), N_EXPERTS - 1).astype(jnp.int32)
    n_used = (ends[-1:] // MOE_ROWS).astype(jnp.int32)
    ys = _experts(h, block_e, n_used, slot_tok, w_gu, b_gu, w_down, b_down, l)
    return _combine(dest, ys, wt, x, mod4, seq)


def _pad_lanes(v):
    return jnp.pad(v, ((0, 0), (0, LANES - v.shape[-1])))[:, None, :]


def kernel(x, c, w_mod, b_mod, g_norm_mix, g_norm_ffn, w_in, hg_lower_bound, hg_onorm, s5_lambda_re, s5_lambda_im, s5_log_dt, s5_b_re, s5_b_im, s5_c_re, s5_c_im, s5_d, s5_w_glu, s5_b_glu, m2_conv_w, m2_conv_b, m2_dt_bias, m2_a_log, m2_d, m2_norm, rk_mu, rk_w0, rk_w2, rk_a0, rk_a2, rk_g2, rk_k_k, rk_k_a, rk_r_k, rk_ln_w, rk_ln_b, w_branch, w_out, w_router, b_router, w_gu, b_gu, w_down, b_down, g_final):
    batch, seq, d = x.shape
    depth = w_in.shape[0]
    n = batch * seq
    dt0 = N_MAIN
    w_tail = jnp.concatenate([w_in[:, :, dt0 + M2_HEADS:], w_in[:, :, dt0:dt0 + M2_HEADS],
                              jnp.zeros((depth, d, LANES - M2_HEADS), F32)], axis=-1)
    w_branch_b = w_branch.astype(BF16)
    w_out_b = w_out.astype(BF16)
    w_gu_b = w_gu.astype(BF16)
    w_down_b = w_down.astype(BF16)
    w_router_p = jnp.pad(w_router, ((0, 0), (0, 0), (0, LANES - N_EXPERTS)))
    b_router_p = _pad_lanes(b_router)
    b_gu4 = b_gu[:, :, None, :]
    b_down4 = b_down[:, :, None, :]
    lbs = jax.nn.softmax(hg_lower_bound.astype(F32), axis=0)
    lbs = jnp.cumsum(lbs, axis=0) - lbs[0]
    dt_bias_p = _pad_lanes(m2_dt_bias)
    a_log_p = _pad_lanes(m2_a_log)
    m2_d_rep = jnp.repeat(m2_d, M2_HEADDIM, axis=-1)[:, None, :]
    c_pad = jnp.pad(c, ((0, 8 - batch), (0, 0)))
    s5_tables = jax.vmap(functools.partial(_s5_tables, n_chunks=seq // S5_T))(
        s5_lambda_re, s5_lambda_im, s5_log_dt, s5_b_re, s5_b_im, s5_c_re, s5_c_im)

    x2 = x.reshape(n, d)
    for l in range(depth):
        mod = _modulation(c_pad, w_mod, b_mod[:, None, :], l)
        mod4 = mod[:batch].reshape(batch, 6, 1, d)
        h = _norm_modulate(x2, g_norm_mix[:, None, :], l, mod4, 0, 1, seq, BF16)
        proj = _matmul(h, w_in, l, N_MAIN, 512, 1024, F32, "in_proj_main")
        tail = _matmul(h, w_tail, l, N_TAIL, 512, 640, F32, "in_proj_tail")
        ya = _hgrn2(proj, lbs[l][None], hg_onorm[l][None], batch, seq)
        yb = _s5(proj, s5_tables, l, s5_d[l][None], s5_w_glu[l].astype(BF16), s5_b_glu[l][None], batch, seq)
        yc = _mamba2(proj, tail, m2_conv_w[l], m2_conv_b[l][None], dt_bias_p[l], a_log_p[l], m2_d_rep[l],
                     m2_norm[l][None], batch, seq)
        yd = _rwkv7(tail, rk_mu[l][None], rk_w0[l][None], rk_w2[l], rk_a0[l][None], rk_a2[l], rk_g2[l],
                    rk_k_k[l][None], rk_k_a[l][None], rk_r_k[l].reshape(1, BRANCH), rk_ln_w[l][None],
                    rk_ln_b[l][None], batch, seq)
        merged = _merge([ya, yb, yc, yd], proj, w_branch_b, l)
        x2 = _out_proj_residual(merged, w_out_b, l, x2, mod4, 2, seq)
        x2 = _moe(x2, g_norm_ffn[:, None, :], l, mod4, w_router_p, b_router_p, w_gu_b, b_gu4,
                  w_down_b, b_down4, seq)
    return _final_norm(x2, g_final[None]).reshape(batch, seq, d)
```

```python
import functools
import math

import jax
import jax.numpy as jnp
from jax import lax
from jax.experimental import pallas as pl
from jax.experimental.pallas import tpu as pltpu

F32 = jnp.float32
BF16 = jnp.bfloat16
HI = lax.Precision.HIGHEST

D_MODEL = 2048
CHUNK = 64
BRANCH = 512
HG_HEADS, HG_DK = 4, 128
S5_CH, S5_GROUPS, S5_STATE = 16, 32, 64
S5_T = 8
S5_PACK = 8
M2_HEADS, M2_HEADDIM, M2_GROUPS, M2_STATE, M2_CONV, M2_XBC = 8, 64, 2, 128, 4, 1024
RK_HEADS, RK_HEADSIZE = 8, 64
RK_DECAY_LORA, RK_A_LORA, RK_GATE_LORA, RK_WIDTH = 64, 64, 128, 1792
RK_GN_EPS = 64e-5
GATE_IN = 4 * D_MODEL
N_MAIN = GATE_IN + 4 * BRANCH + BRANCH + BRANCH + M2_XBC
N_TAIL = RK_WIDTH + 128
N_EXPERTS, TOP_K, D_EXPERT = 32, 4, 768
SWIGLU_LIMIT, SWIGLU_ALPHA = 7.0, 1.702
MOE_ROWS = 256
NORM_EPS = 1e-6
LANES = 128
VMEM_LIMIT = 48 * 1024 * 1024


def _dot(a, b, prec=None):
    return jnp.dot(a, b, preferred_element_type=F32, precision=prec)


def _dot_nt(a, b, prec=None):
    return lax.dot_general(a, b, (((1,), (1,)), ((), ())), preferred_element_type=F32, precision=prec)


def _dot_tn(a, b, prec=None):
    return lax.dot_general(a, b, (((0,), (0,)), ((), ())), preferred_element_type=F32, precision=prec)


def _bdot(a, b):
    return _dot(a.astype(BF16), b.astype(BF16))


def _bdot_nt(a, b):
    return _dot_nt(a.astype(BF16), b.astype(BF16))


def _bdot_tn(a, b):
    return _dot_tn(a.astype(BF16), b.astype(BF16))


def _sigmoid(x):
    return 1.0 / (1.0 + jnp.exp(-x))


def _silu(x):
    return x * _sigmoid(x)


def _softplus(x):
    return jnp.maximum(x, 0.0) + jnp.log(1.0 + jnp.exp(-jnp.abs(x)))


def _tril(n, k=0):
    r = lax.broadcasted_iota(jnp.int32, (n, n), 0)
    c = lax.broadcasted_iota(jnp.int32, (n, n), 1)
    return (c - r) <= k


def _params(*sem):
    return pltpu.CompilerParams(dimension_semantics=sem, vmem_limit_bytes=VMEM_LIMIT)


def _mm_kernel(x_ref, w_ref, o_ref, wbf_ref):
    @pl.when(pl.program_id(1) == 0)
    def _():
        wbf_ref[...] = w_ref[...].astype(BF16)

    o_ref[...] = _dot(x_ref[...], wbf_ref[...]).astype(o_ref.dtype)


def _mod_kernel(c_ref, w_ref, b_ref, o_ref):
    o_ref[...] = _dot(_silu(c_ref[...]).astype(BF16), w_ref[...].astype(BF16)) + b_ref[...]


def _mm_resid_kernel(x_ref, w_ref, r_ref, g_ref, o_ref):
    o_ref[...] = r_ref[...] + g_ref[...] * _dot(x_ref[...], w_ref[...])


def _matmul(x, w, l, n_cols, tm, tn, out_dtype, name):
    m, k = x.shape
    return pl.pallas_call(
        _mm_kernel,
        grid=(n_cols // tn, m // tm),
        in_specs=[pl.BlockSpec((tm, k), lambda j, i: (i, 0)),
                  pl.BlockSpec((None, k, tn), lambda j, i: (l, 0, j))],
        out_specs=pl.BlockSpec((tm, tn), lambda j, i: (i, j)),
        out_shape=jax.ShapeDtypeStruct((m, n_cols), out_dtype),
        scratch_shapes=[pltpu.VMEM((k, tn), BF16)],
        compiler_params=_params("parallel", "arbitrary"),
        name=name,
    )(x, w)


def _modulation(cond, w_mod, b_mod, l):
    m, k = cond.shape
    n = w_mod.shape[-1]
    tn = 1024
    return pl.pallas_call(
        _mod_kernel,
        grid=(n // tn,),
        in_specs=[pl.BlockSpec((m, k), lambda j: (0, 0)),
                  pl.BlockSpec((None, k, tn), lambda j: (l, 0, j)),
                  pl.BlockSpec((None, 1, tn), lambda j: (l, 0, j))],
        out_specs=pl.BlockSpec((m, tn), lambda j: (0, j)),
        out_shape=jax.ShapeDtypeStruct((m, n), F32),
        compiler_params=_params("parallel"),
        name="modulation",
    )(cond, w_mod, b_mod)


def _out_proj_residual(merged, w_out, l, x, mod4, gate_idx, seq):
    m, k = merged.shape
    tm, tn = 512, 512
    per_b = seq // tm
    return pl.pallas_call(
        _mm_resid_kernel,
        grid=(D_MODEL // tn, m // tm),
        in_specs=[pl.BlockSpec((tm, k), lambda j, i: (i, 0)),
                  pl.BlockSpec((None, k, tn), lambda j, i: (l, 0, j)),
                  pl.BlockSpec((tm, tn), lambda j, i: (i, j)),
                  pl.BlockSpec((None, None, 1, tn), lambda j, i: (i // per_b, gate_idx, 0, j))],
        out_specs=pl.BlockSpec((tm, tn), lambda j, i: (i, j)),
        out_shape=jax.ShapeDtypeStruct((m, D_MODEL), F32),
        compiler_params=_params("parallel", "parallel"),
        name="out_proj_residual",
    )(merged, w_out, x, mod4)


def _rms(x, g):
    return x * lax.rsqrt(jnp.mean(x * x, axis=-1, keepdims=True) + NORM_EPS) * g


def _normmod_kernel(x_ref, g_ref, sc_ref, sh_ref, o_ref):
    y = _rms(x_ref[...], g_ref[...])
    o_ref[...] = (y * (1.0 + sc_ref[...]) + sh_ref[...]).astype(o_ref.dtype)


def _norm_modulate(x, g, l, mod4, shift_idx, scale_idx, seq, out_dtype):
    m, d = x.shape
    ts = 512
    per_b = seq // ts
    return pl.pallas_call(
        _normmod_kernel,
        grid=(m // ts,),
        in_specs=[pl.BlockSpec((ts, d), lambda i: (i, 0)),
                  pl.BlockSpec((None, 1, d), lambda i: (l, 0, 0)),
                  pl.BlockSpec((None, None, 1, d), lambda i: (i // per_b, scale_idx, 0, 0)),
                  pl.BlockSpec((None, None, 1, d), lambda i: (i // per_b, shift_idx, 0, 0))],
        out_specs=pl.BlockSpec((ts, d), lambda i: (i, 0)),
        out_shape=jax.ShapeDtypeStruct((m, d), out_dtype),
        compiler_params=_params("parallel"),
        name="norm_modulate",
    )(x, g, mod4, mod4)


def _final_norm_kernel(x_ref, g_ref, o_ref):
    o_ref[...] = _rms(x_ref[...], g_ref[...])


def _final_norm(x, g):
    m, d = x.shape
    ts = 512
    return pl.pallas_call(
        _final_norm_kernel,
        grid=(m // ts,),
        in_specs=[pl.BlockSpec((ts, d), lambda i: (i, 0)), pl.BlockSpec((1, d), lambda i: (0, 0))],
        out_specs=pl.BlockSpec((ts, d), lambda i: (i, 0)),
        out_shape=jax.ShapeDtypeStruct((m, d), F32),
        compiler_params=_params("parallel"),
        name="final_norm",
    )(x, g)


HG_SUB = 16


def _hgrn2_kernel(q_ref, f_ref, i_ref, g_ref, lb_ref, on_ref, o_ref, state_ref):
    @pl.when(pl.program_id(1) == 0)
    def _():
        state_ref[...] = jnp.zeros_like(state_ref)

    sub_mask = _tril(HG_SUB)
    lb = lb_ref[...]
    forget = lb + (1.0 - lb) * _sigmoid(f_ref[...])
    k = 1.0 - forget
    q = _silu(q_ref[...])
    v = i_ref[...]
    cum = _dot(_tril(CHUNK).astype(F32), jnp.log(forget), HI)
    last = cum[CHUNK - 1:CHUNK, :]
    q_dec = q * jnp.exp(cum)
    k_end = k * jnp.exp(last - cum)
    heads = range(HG_HEADS)
    sls = [slice(h * HG_DK, (h + 1) * HG_DK) for h in heads]
    sts = [state_ref[h] for h in heads]
    o_inter = [_bdot_nt(q_dec[:, sls[h]], sts[h]) for h in heads]
    for h in heads:
        state_ref[h] = sts[h] * jnp.exp(last[:, sls[h]]) + _bdot_tn(v[:, sls[h]], k_end[:, sls[h]])
    rows = [[None] * (CHUNK // HG_SUB) for _ in heads]
    for i in range(CHUNK // HG_SUB):
        r0 = i * HG_SUB
        if i > 0:
            ci = cum[r0 - 1:r0, :]
            qp = q[r0:r0 + HG_SUB] * jnp.exp(cum[r0:r0 + HG_SUB] - ci)
            kp = k[:r0] * jnp.exp(ci - cum[:r0])
            sc = [_bdot_nt(qp[:, sls[h]], kp[:, sls[h]]) for h in heads]
            off = [_bdot(sc[h], v[:r0, sls[h]]) for h in heads]
        for h in heads:
            sl = sls[h]
            qi = q[r0:r0 + HG_SUB, sl]
            cumi = cum[r0:r0 + HG_SUB, sl]
            oi = o_inter[h][r0:r0 + HG_SUB]
            if i > 0:
                oi = oi + off[h]
            for s in range(HG_SUB):
                diff = jnp.where(sub_mask[:, s:s + 1], cumi - cumi[s:s + 1, :], -jnp.inf)
                w = jnp.sum(qi * jnp.exp(diff) * k[r0 + s:r0 + s + 1, sl], axis=-1, keepdims=True)
                oi = oi + w * v[r0 + s:r0 + s + 1, sl]
            rows[h][i] = oi
    outs = [_rms(jnp.concatenate(rows[h], axis=0), on_ref[...]) for h in heads]
    o_ref[...] = jnp.concatenate(outs, axis=-1) * _silu(g_ref[...])


def _hgrn2(proj, lb, onorm, batch, seq):
    nc = seq // CHUNK
    col0 = GATE_IN // BRANCH
    spec = lambda off: pl.BlockSpec((CHUNK, BRANCH), lambda b, c: (b * nc + c, col0 + off))
    vec = lambda n: pl.BlockSpec((1, n), lambda b, c: (0, 0))
    return pl.pallas_call(
        _hgrn2_kernel,
        grid=(batch, nc),
        in_specs=[spec(0), spec(1), spec(2), spec(3), vec(BRANCH), vec(HG_DK)],
        out_specs=pl.BlockSpec((CHUNK, BRANCH), lambda b, c: (b * nc + c, 0)),
        out_shape=jax.ShapeDtypeStruct((batch * seq, BRANCH), F32),
        scratch_shapes=[pltpu.VMEM((HG_HEADS, HG_DK, HG_DK), F32)],
        compiler_params=_params("parallel", "arbitrary"),
        name="hgrn2",
    )(proj, proj, proj, proj, lb, onorm)


def _mamba2_kernel(z_ref, xbc_ref, dt_ref, cw_ref, cb_ref, dtb_ref, alog_ref, d_ref, ng_ref,
                   o_ref, ext_ref, state_ref):
    @pl.when(pl.program_id(1) == 0)
    def _():
        ext_ref[0:8, :] = jnp.zeros((8, M2_XBC), F32)
        state_ref[...] = jnp.zeros_like(state_ref)

    ext_ref[8:8 + CHUNK, :] = xbc_ref[...]
    conv = cb_ref[...]
    for j in range(M2_CONV):
        conv = conv + cw_ref[j:j + 1, :] * ext_ref[8 - (M2_CONV - 1) + j:8 - (M2_CONV - 1) + j + CHUNK, :]
    ext_ref[0:8, :] = xbc_ref[CHUNK - 8:CHUNK, :]
    xa = _silu(conv)
    gn = M2_GROUPS * M2_STATE
    xs = xa[:, :BRANCH]
    dt = _softplus(dt_ref[...] + dtb_ref[...])
    adt = -jnp.exp(alog_ref[...]) * dt
    tri = _tril(CHUNK)
    cum = _dot(tri.astype(F32), adt, HI)
    cum_t = cum.T
    cb = []
    for g in range(M2_GROUPS):
        bm = xa[:, BRANCH + g * M2_STATE:BRANCH + (g + 1) * M2_STATE]
        cm = xa[:, BRANCH + gn + g * M2_STATE:BRANCH + gn + (g + 1) * M2_STATE]
        cb.append((bm, cm, _bdot_nt(cm, bm)))
    ys = []
    for h in range(M2_HEADS):
        bm, cm, cbg = cb[h // (M2_HEADS // M2_GROUPS)]
        cum_h = cum[:, h:h + 1]
        lmat = jnp.exp(jnp.where(tri, cum_h - cum_t[h:h + 1, :], -jnp.inf))
        x_h = xs[:, h * M2_HEADDIM:(h + 1) * M2_HEADDIM]
        xdt = x_h * dt[:, h:h + 1]
        st = state_ref[h]
        y = _bdot(cbg * lmat, xdt) + jnp.exp(cum_h) * _bdot(cm, st)
        last = cum[CHUNK - 1:CHUNK, h:h + 1]
        state_ref[h] = jnp.exp(last) * st + _bdot_tn(bm * jnp.exp(last - cum_h), xdt)
        ys.append(y)
    y = jnp.concatenate(ys, axis=-1) + d_ref[...] * xs
    y = y * _silu(z_ref[...])
    gw = BRANCH // M2_GROUPS
    outs = [_rms(y[:, g * gw:(g + 1) * gw], ng_ref[:, g * gw:(g + 1) * gw]) for g in range(M2_GROUPS)]
    o_ref[...] = jnp.concatenate(outs, axis=-1)


def _mamba2(proj, tail, conv_w, conv_b, dt_bias, a_log, d_rep, norm_g, batch, seq):
    nc = seq // CHUNK
    zcol = (GATE_IN + 5 * BRANCH) // BRANCH
    xcol = (GATE_IN + 6 * BRANCH) // M2_XBC
    vec = lambda r, n: pl.BlockSpec((r, n), lambda b, c: (0, 0))
    return pl.pallas_call(
        _mamba2_kernel,
        grid=(batch, nc),
        in_specs=[pl.BlockSpec((CHUNK, BRANCH), lambda b, c: (b * nc + c, zcol)),
                  pl.BlockSpec((CHUNK, M2_XBC), lambda b, c: (b * nc + c, xcol)),
                  pl.BlockSpec((CHUNK, LANES), lambda b, c: (b * nc + c, RK_WIDTH // LANES)),
                  vec(M2_CONV, M2_XBC), vec(1, M2_XBC), vec(1, LANES), vec(1, LANES),
                  vec(1, BRANCH), vec(1, BRANCH)],
        out_specs=pl.BlockSpec((CHUNK, BRANCH), lambda b, c: (b * nc + c, 0)),
        out_shape=jax.ShapeDtypeStruct((batch * seq, BRANCH), F32),
        scratch_shapes=[pltpu.VMEM((8 + CHUNK, M2_XBC), F32),
                        pltpu.VMEM((M2_HEADS, M2_STATE, M2_HEADDIM), F32)],
        compiler_params=_params("parallel", "arbitrary"),
        name="mamba2",
    )(proj, proj, tail, conv_w, conv_b, dt_bias, a_log, d_rep, norm_g)


def _unit_lower_inverses(mats):
    n = mats[0].shape[0]
    eye = (lax.broadcasted_iota(jnp.int32, (n, n), 0) == lax.broadcasted_iota(jnp.int32, (n, n), 1)).astype(F32)
    invs = [eye - a for a in mats]
    pows = list(mats)
    k = 2
    while k < n:
        pows = [_bdot(p, p) for p in pows]
        invs = [_bdot(inv, eye + p) for inv, p in zip(invs, pows)]
        k *= 2
    return invs


def _rwkv7_kernel(p_ref, mu_ref, w0_ref, w2_ref, a0_ref, a2_ref, g2_ref, kk_ref, ka_ref, rk_ref,
                  lnw_ref, lnb_ref, o_ref, ext_ref, state_ref):
    @pl.when(pl.program_id(1) == 0)
    def _():
        ext_ref[0:8, :] = jnp.zeros((8, RK_WIDTH), F32)
        state_ref[...] = jnp.zeros_like(state_ref)

    p = p_ref[...]
    ext_ref[8:8 + CHUNK, :] = p
    prev = ext_ref[7:7 + CHUNK, :]
    ext_ref[0:8, :] = p_ref[CHUNK - 8:CHUNK, :]
    pm = p + (prev - p) * mu_ref[...]
    b3 = 3 * BRANCH
    r = pm[:, :BRANCH]
    k = pm[:, BRANCH:2 * BRANCH]
    v = pm[:, 2 * BRANCH:b3]
    w_lo = pm[:, b3:b3 + RK_DECAY_LORA]
    a_lo = pm[:, b3 + RK_DECAY_LORA:b3 + RK_DECAY_LORA + RK_A_LORA]
    g_lo = pm[:, b3 + RK_DECAY_LORA + RK_A_LORA:]
    w = -_softplus(-(w0_ref[...] + _bdot(jnp.tanh(w_lo), w2_ref[...]))) - 0.5
    logw = -jnp.exp(w)
    a = _sigmoid(a0_ref[...] + _bdot(a_lo, a2_ref[...]))
    g = _bdot(_sigmoid(g_lo), g2_ref[...])
    kk = k * kk_ref[...]
    k = k * (1.0 + (a - 1.0) * ka_ref[...])
    c, hs = CHUNK, RK_HEADSIZE
    row = lax.broadcasted_iota(jnp.int32, (2 * c, 2 * c), 0)
    col = lax.broadcasted_iota(jnp.int32, (2 * c, 2 * c), 1)
    quad_mask = (col & (c - 1)) < (row & (c - 1)) + (row >= c).astype(jnp.int32)
    cum = _dot(_tril(c).astype(F32), logw, HI)
    last = cum[c - 1:c, :]
    gam = jnp.exp(cum)
    ginv = jnp.exp(-cum)
    to_end = jnp.exp(last - cum)
    gam_prev = jnp.exp(cum - logw)
    heads = range(RK_HEADS)
    sls = [slice(h * hs, (h + 1) * hs) for h in heads]
    kap = []
    for sl in sls:
        kh = kk[:, sl]
        kap.append(kh / jnp.maximum(jnp.sqrt(jnp.sum(kh * kh, axis=-1, keepdims=True)), 1e-12))
    beta = [kap[h] * a[:, sls[h]] for h in heads]
    lhs = [jnp.concatenate([kap[h] * gam_prev[:, sls[h]], r[:, sls[h]] * gam[:, sls[h]]], axis=0)
           for h in heads]
    rhs = [jnp.concatenate([beta[h] * ginv[:, sls[h]], k[:, sls[h]] * ginv[:, sls[h]]], axis=0)
           for h in heads]
    quad = [jnp.where(quad_mask, _bdot_nt(lhs[h], rhs[h]), 0.0) for h in heads]
    invs = _unit_lower_inverses([q[:c, :c] for q in quad])
    vs = [v[:, sl] for sl in sls]
    akv = [_bdot(quad[h][:, c:], vs[h]) for h in heads]
    sts = [state_ref[h] for h in heads]
    from_state = [_bdot_nt(lhs[h], sts[h]) for h in heads]
    us = [_bdot(invs[h], -(from_state[h][:c] + akv[h][:c])) for h in heads]
    ys = [from_state[h][c:] + akv[h][c:] + _bdot(quad[h][c:, :c], us[h]) for h in heads]
    for h in heads:
        sl = sls[h]
        upd = _bdot_tn(jnp.concatenate([us[h], vs[h]], axis=0),
                       jnp.concatenate([beta[h] * to_end[:, sl], k[:, sl] * to_end[:, sl]], axis=0))
        state_ref[h] = sts[h] * gam[c - 1:c, sl] + upd
    outs = []
    for h in heads:
        sl = sls[h]
        mean = jnp.mean(ys[h], axis=-1, keepdims=True)
        yc = ys[h] - mean
        var = jnp.mean(yc * yc, axis=-1, keepdims=True)
        yn = yc * lax.rsqrt(var + RK_GN_EPS) * lnw_ref[:, sl] + lnb_ref[:, sl]
        bonus = jnp.sum(r[:, sl] * k[:, sl] * rk_ref[:, sl], axis=-1, keepdims=True) * vs[h]
        outs.append(yn + bonus)
    o_ref[...] = jnp.concatenate(outs, axis=-1) * g


def _rwkv7(tail, mu, w0, w2, a0, a2, g2, k_k, k_a, r_k, ln_w, ln_b, batch, seq):
    nc = seq // CHUNK
    vec = lambda r, n: pl.BlockSpec((r, n), lambda b, c: (0, 0))
    return pl.pallas_call(
        _rwkv7_kernel,
        grid=(batch, nc),
        in_specs=[pl.BlockSpec((CHUNK, RK_WIDTH), lambda b, c: (b * nc + c, 0)),
                  vec(1, RK_WIDTH), vec(1, BRANCH), vec(RK_DECAY_LORA, BRANCH), vec(1, BRANCH),
                  vec(RK_A_LORA, BRANCH), vec(RK_GATE_LORA, BRANCH), vec(1, BRANCH), vec(1, BRANCH),
                  vec(1, BRANCH), vec(1, BRANCH), vec(1, BRANCH)],
        out_specs=pl.BlockSpec((CHUNK, BRANCH), lambda b, c: (b * nc + c, 0)),
        out_shape=jax.ShapeDtypeStruct((batch * seq, BRANCH), F32),
        scratch_shapes=[pltpu.VMEM((8 + CHUNK, RK_WIDTH), F32),
                        pltpu.VMEM((RK_HEADS, RK_HEADSIZE, RK_HEADSIZE), F32)],
        compiler_params=_params("parallel", "arbitrary"),
        name="rwkv7",
    )(tail, mu, w0, w2, a0, a2, g2, k_k, k_a, r_k, ln_w, ln_b)


def _s5_tables(lam_re, lam_im, log_dt, b_re, b_im, c_re, c_im, n_chunks):
    t = S5_T
    g, p, i = b_re.shape
    dt = jnp.exp(log_dt)[:, None]

    def apow(n):
        e = n.astype(F32)[..., None, None]
        mag = jnp.exp(e * lam_re * dt)
        return mag * jnp.cos(e * lam_im * dt), mag * jnp.sin(e * lam_im * dt)

    ab_re, ab_im = apow(jnp.ones((), F32))
    den = lam_re * lam_re + lam_im * lam_im
    nr = ab_re - 1.0
    coef_re = ((nr * lam_re + ab_im * lam_im) / den)[..., None]
    coef_im = ((ab_im * lam_re - nr * lam_im) / den)[..., None]
    bt_re = coef_re * b_re - coef_im * b_im
    bt_im = coef_re * b_im + coef_im * b_re
    ar, ai = apow(jnp.arange(t + 1))
    cr, ci = c_re[None], c_im[None]
    ca_re = cr * ar[:, :, None, :] - ci * ai[:, :, None, :]
    ca_im = cr * ai[:, :, None, :] + ci * ar[:, :, None, :]
    kern = (jnp.einsum('tgop,gpi->tgoi', ca_re[:t], bt_re, precision=HI)
            - jnp.einsum('tgop,gpi->tgoi', ca_im[:t], bt_im, precision=HI))
    lag = jnp.arange(t)[None, :] - jnp.arange(t)[:, None]
    toep = jnp.where((lag >= 0)[:, :, None, None, None], kern[jnp.maximum(lag, 0)], 0.0)
    toep = toep.transpose(2, 0, 4, 1, 3).reshape(g, t * i, t * i)
    br, bi = ar[:t][::-1], ai[:t][::-1]
    bs_re = br[..., None] * bt_re - bi[..., None] * bt_im
    bs_im = br[..., None] * bt_im + bi[..., None] * bt_re
    bst = jnp.concatenate([bs_re, bs_im], axis=2).transpose(1, 0, 3, 2).reshape(g, t * i, 2 * p)
    cst = jnp.concatenate([ca_re[1:], -ca_im[1:]], axis=3)
    cst = cst.transpose(1, 3, 0, 2).reshape(g, 2 * p, t * i)
    levels = max(1, int(math.log2(n_chunks)))
    pr, pi = apow(t * (2 ** jnp.arange(levels)))
    m1 = jnp.concatenate([pr, pr], axis=-1)
    m2 = jnp.concatenate([-pi, pi], axis=-1)
    k = S5_PACK
    a = g // k
    eye = jnp.eye(k, dtype=F32)
    w = k * i
    place = (eye[:, None, :, None] * jnp.eye(i, dtype=F32)[None, :, None, :]).reshape(k, i, w)
    toep_p = jnp.einsum('agsizo,goc->asgizc', toep.reshape(a, k, t, i, t, i), place).reshape(a, t * w, t * w)
    bst_p = jnp.einsum('agsip,gh->asgihp', bst.reshape(a, k, t, i, 2 * p), eye).reshape(a, t * w, k * 2 * p)
    cst_p = jnp.einsum('agpzo,goc->agpzc', cst.reshape(a, k, 2 * p, t, i), place).reshape(a, k * 2 * p, t * w)
    m1 = m1.reshape(levels, a, 1, k * 2 * p)
    m2 = m2.reshape(levels, a, 1, k * 2 * p)
    return toep_p.astype(BF16), bst_p.astype(BF16), cst_p.astype(BF16), m1, m2


def _s5_kernel(n_chunks, u_ref, toep_ref, bst_ref, cst_ref, m1_ref, m2_ref, y_ref, u_scr):
    for s in range(S5_T):
        u_scr[:, s * LANES:(s + 1) * LANES] = u_ref[:, s, :].astype(BF16)
    u = u_scr[...]
    y = _dot(u, toep_ref[...])
    x = _dot(u, bst_ref[...])
    rows = lax.broadcasted_iota(jnp.int32, x.shape, 0) & (n_chunks - 1)
    lane = lax.broadcasted_iota(jnp.int32, x.shape, 1)
    re_half = (lane & (2 * S5_STATE - 1)) < S5_STATE
    width = x.shape[1]

    def shifted(val, d):
        return jnp.where(rows >= d, pltpu.roll(val, d, axis=0), 0.0)

    def swap_re_im(val):
        return jnp.where(re_half, pltpu.roll(val, width - S5_STATE, axis=1), pltpu.roll(val, S5_STATE, axis=1))

    d = 1
    lvl = 0
    while d < n_chunks:
        xs = shifted(x, d)
        x = x + xs * m1_ref[lvl] + swap_re_im(xs) * m2_ref[lvl]
        d *= 2
        lvl += 1
    y = y + _dot(shifted(x, 1).astype(BF16), cst_ref[...])
    for s in range(S5_T):
        y_ref[:, s, :] = y[:, s * LANES:(s + 1) * LANES]


def _s5_scan(proj3, toep, bst, cst, m1, m2, l, n_chunks):
    r = proj3.shape[0]
    a, w = toep.shape[1], toep.shape[2]
    levels = m1.shape[1]
    ucol = (GATE_IN + 4 * BRANCH) // LANES
    tab = lambda: pl.BlockSpec((None, None, w, w), lambda i: (l, i, 0, 0))
    vec = lambda: pl.BlockSpec((None, levels, None, 1, w), lambda i: (l, 0, i, 0, 0))
    return pl.pallas_call(
        functools.partial(_s5_kernel, n_chunks),
        grid=(a,),
        in_specs=[pl.BlockSpec((r, S5_T, LANES), lambda i: (0, 0, ucol + i)), tab(), tab(), tab(), vec(), vec()],
        out_specs=pl.BlockSpec((r, S5_T, LANES), lambda i: (0, 0, i)),
        out_shape=jax.ShapeDtypeStruct((r, S5_T, BRANCH), F32),
        scratch_shapes=[pltpu.VMEM((r, w), BF16)],
        compiler_params=_params("parallel"),
        name="s5_scan",
    )(proj3, toep, bst, cst, m1, m2)


def _s5_post_kernel(y_ref, u_ref, d_ref, w_ref, b_ref, o_ref):
    y = y_ref[...] + d_ref[...] * u_ref[...]
    y = 0.5 * y * (1.0 + jnp.tanh(math.sqrt(2.0 / math.pi) * (y + 0.044715 * (y * y * y))))
    o_ref[...] = y * _sigmoid(_bdot(y, w_ref[...]) + b_ref[...])


def _s5_post(y_ssm, proj, d_skip, w_glu, b_glu):
    m = y_ssm.shape[0]
    tm = min(512, m)
    ucol = (GATE_IN + 4 * BRANCH) // BRANCH
    return pl.pallas_call(
        _s5_post_kernel,
        grid=(m // tm,),
        in_specs=[pl.BlockSpec((tm, BRANCH), lambda i: (i, 0)),
                  pl.BlockSpec((tm, BRANCH), lambda i: (i, ucol)),
                  pl.BlockSpec((1, BRANCH), lambda i: (0, 0)),
                  pl.BlockSpec((BRANCH, BRANCH), lambda i: (0, 0)),
                  pl.BlockSpec((1, BRANCH), lambda i: (0, 0))],
        out_specs=pl.BlockSpec((tm, BRANCH), lambda i: (i, 0)),
        out_shape=jax.ShapeDtypeStruct((m, BRANCH), F32),
        compiler_params=_params("parallel"),
        name="s5_post",
    )(y_ssm, proj, d_skip, w_glu, b_glu)


def _s5(proj, tables, l, d_skip, w_glu, b_glu, batch, seq):
    n = batch * seq
    y3 = _s5_scan(proj.reshape(n // S5_T, S5_T, proj.shape[1]), *tables, l, seq // S5_T)
    return _s5_post(y3.reshape(n, BRANCH), proj, d_skip, w_glu, b_glu)


def _merge_kernel(ya_ref, yb_ref, yc_ref, yd_ref, ga_ref, gb_ref, gc_ref, gd_ref,
                  wa_ref, wb_ref, wc_ref, wd_ref, o_ref):
    acc = None
    for y_ref, g_ref, w_ref in ((ya_ref, ga_ref, wa_ref), (yb_ref, gb_ref, wb_ref),
                                (yc_ref, gc_ref, wc_ref), (yd_ref, gd_ref, wd_ref)):
        term = _sigmoid(g_ref[...]) * _dot(y_ref[...].astype(BF16), w_ref[...])
        acc = term if acc is None else acc + term
    o_ref[...] = acc.astype(o_ref.dtype)


def _merge(ys, proj, w_branch, l):
    m = ys[0].shape[0]
    tm = 256
    yspec = pl.BlockSpec((tm, BRANCH), lambda i: (i, 0))
    gspec = lambda n: pl.BlockSpec((tm, D_MODEL), lambda i: (i, n))
    wspec = lambda n: pl.BlockSpec((None, None, BRANCH, D_MODEL), lambda i: (l, n, 0, 0))
    return pl.pallas_call(
        _merge_kernel,
        grid=(m // tm,),
        in_specs=[yspec] * 4 + [gspec(n) for n in range(4)] + [wspec(n) for n in range(4)],
        out_specs=pl.BlockSpec((tm, D_MODEL), lambda i: (i, 0)),
        out_shape=jax.ShapeDtypeStruct((m, D_MODEL), BF16),
        compiler_params=_params("parallel"),
        name="branch_merge",
    )(*ys, proj, proj, proj, proj, w_branch, w_branch, w_branch, w_branch)


ROUTER_TM = 256
HALF_D = D_MODEL // 2


def _pack_bf16_halves(lo, hi):
    lo_bits = pltpu.bitcast(lo.astype(BF16).astype(F32), jnp.uint32)
    hi_bits = pltpu.bitcast(hi.astype(BF16).astype(F32), jnp.uint32)
    return (hi_bits & jnp.uint32(0xFFFF0000)) | (lo_bits >> 16)


def _pack_bf16_pairs(v):
    return _pack_bf16_halves(v[:, :HALF_D], v[:, HALF_D:])


def _unpack_bf16_pairs(p):
    lo = pltpu.bitcast(p << 16, F32)
    hi = pltpu.bitcast(p & jnp.uint32(0xFFFF0000), F32)
    return lo, hi


def _router_kernel(x_ref, g_ref, sc_ref, sh_ref, wr_ref, br_ref,
                   h_ref, idx_ref, wt_ref, rank_ref, cnt_ref, carry_ref):
    @pl.when(pl.program_id(0) == 0)
    def _():
        carry_ref[...] = jnp.zeros_like(carry_ref)

    h = _rms(x_ref[...], g_ref[...]) * (1.0 + sc_ref[...]) + sh_ref[...]
    h_ref[...] = _pack_bf16_pairs(h)
    tm = h.shape[0]
    lane = lax.broadcasted_iota(jnp.int32, (tm, LANES), 1)
    logits = _dot(h, wr_ref[...], HI) + br_ref[...]
    masked = jnp.where(lane < N_EXPERTS, logits, -jnp.inf)
    vals, hots, idxs = [], [], []
    for _ in range(TOP_K):
        m = jnp.max(masked, axis=-1, keepdims=True)
        idx = jnp.min(jnp.where(masked == m, lane, LANES), axis=-1, keepdims=True)
        hot = lane == idx
        masked = jnp.where(hot, -jnp.inf, masked)
        vals.append(m)
        hots.append(hot)
        idxs.append(idx)
    exps = [jnp.exp(v - vals[0]) for v in vals]
    denom = exps[0] + exps[1] + exps[2] + exps[3]
    multi = sum(hot.astype(F32) for hot in hots)
    before = _dot(_tril(tm, -1).astype(BF16), multi.astype(BF16)) + carry_ref[...]
    carry_ref[...] = carry_ref[...] + jnp.sum(multi, axis=0, keepdims=True)
    idx_out = jnp.zeros((tm, LANES), jnp.int32)
    wt_out = jnp.zeros((tm, LANES), F32)
    rank_out = jnp.zeros((tm, LANES), F32)
    for k in range(TOP_K):
        rank = jnp.sum(jnp.where(hots[k], before, 0.0), axis=-1, keepdims=True)
        idx_out = jnp.where(lane == k, idxs[k], idx_out)
        wt_out = jnp.where(lane == k, exps[k] / denom, wt_out)
        rank_out = jnp.where(lane == k, rank, rank_out)
    idx_ref[...] = idx_out
    wt_ref[...] = wt_out
    rank_ref[...] = rank_out.astype(jnp.int32)
    cnt_ref[...] = carry_ref[...].astype(jnp.int32)


def _router(x, g, l, mod4, w_router, b_router, seq):
    m, d = x.shape
    tm = min(ROUTER_TM, m)
    per_b = seq // tm
    tok = lambda dt: jax.ShapeDtypeStruct((m, LANES), dt)
    tspec = pl.BlockSpec((tm, LANES), lambda i: (i, 0))
    return pl.pallas_call(
        _router_kernel,
        grid=(m // tm,),
        in_specs=[pl.BlockSpec((tm, d), lambda i: (i, 0)),
                  pl.BlockSpec((None, 1, d), lambda i: (l, 0, 0)),
                  pl.BlockSpec((None, None, 1, d), lambda i: (i // per_b, 4, 0, 0)),
                  pl.BlockSpec((None, None, 1, d), lambda i: (i // per_b, 3, 0, 0)),
                  pl.BlockSpec((None, d, LANES), lambda i: (l, 0, 0)),
                  pl.BlockSpec((None, 1, LANES), lambda i: (l, 0, 0))],
        out_specs=[pl.BlockSpec((tm, HALF_D), lambda i: (i, 0)), tspec, tspec, tspec,
                   pl.BlockSpec((1, LANES), lambda i: (0, 0))],
        out_shape=[jax.ShapeDtypeStruct((m, HALF_D), jnp.uint32), tok(jnp.int32), tok(F32), tok(jnp.int32),
                   jax.ShapeDtypeStruct((1, LANES), jnp.int32)],
        scratch_shapes=[pltpu.VMEM((1, LANES), F32)],
        compiler_params=_params("arbitrary"),
        name="router",
    )(x, g, mod4, mod4, w_router, b_router)


def _row_copy(src_hbm, row, buf, r, sem):
    return pltpu.make_async_copy(src_hbm.at[pl.ds(row, 1), :], buf.at[pl.ds(r, 1), :], sem)


ISSUE_SPLIT = (64, 160)


def _expert_kernel(layer, be_ref, nu_ref, st_ref, first_ref, nexte_ref, hasnext_ref,
                   h_hbm, wgu_hbm, bgu_ref, wd_hbm, bd_ref, o_ref,
                   xbuf0, xbuf1, x_ref, act_ref, wgu_stage, wd_stage, wgu_ref, wd_ref, sem, wsem):
    i = pl.program_id(0)
    n_used = nu_ref[0]

    def weight_copies(e):
        return (pltpu.make_async_copy(wgu_hbm.at[layer, e], wgu_stage, wsem.at[0]),
                pltpu.make_async_copy(wd_hbm.at[layer, e], wd_stage, wsem.at[1]))

    @pl.when(jnp.logical_and(i == 0, n_used > 0))
    def _():
        for cp in weight_copies(be_ref[0]):
            cp.start()

    @pl.when(jnp.logical_and(i < n_used, first_ref[i] == 1))
    def _():
        for cp in weight_copies(be_ref[i]):
            cp.wait()
        wgu_ref[...] = wgu_stage[...].astype(BF16)
        wd_ref[...] = wd_stage[...].astype(BF16)

        @pl.when(hasnext_ref[i] == 1)
        def _():
            for cp in weight_copies(nexte_ref[i]):
                cp.start()

    def issue(base, r0, count, buf, s):
        for r in range(count):
            _row_copy(h_hbm, st_ref[base + r0 + r], buf, r0 + r, s).start()

    def drain(buf, s):
        for r in range(MOE_ROWS):
            _row_copy(h_hbm, 0, buf, r, s).wait()

    @pl.when(jnp.logical_and(i == 0, n_used > 0))
    def _():
        issue(0, 0, MOE_ROWS, xbuf0, sem.at[0])

    def step(cur, cur_sem, nxt, nxt_sem):
        drain(cur, cur_sem)
        lo, hi = _unpack_bf16_pairs(cur[...])
        x_ref[:, :HALF_D] = lo.astype(BF16)
        x_ref[:, HALF_D:] = hi.astype(BF16)
        nbase = jnp.minimum(i + 1, n_used - 1) * MOE_ROWS
        issue(nbase, 0, ISSUE_SPLIT[0], nxt, nxt_sem)

        @pl.when(n_used > i)
        def _():
            gu = _dot(x_ref[...], wgu_ref[...]) + bgu_ref[...]
            gate = jnp.minimum(gu[:, :D_EXPERT], SWIGLU_LIMIT)
            up = jnp.clip(gu[:, D_EXPERT:], -SWIGLU_LIMIT, SWIGLU_LIMIT)
            act_ref[...] = ((up + 1.0) * gate * _sigmoid(SWIGLU_ALPHA * gate)).astype(BF16)
            issue(nbase, ISSUE_SPLIT[0], ISSUE_SPLIT[1] - ISSUE_SPLIT[0], nxt, nxt_sem)

        @pl.when(n_used >= i + 1)
        def _():
            y = _dot(act_ref[...], wd_ref[...]) + bd_ref[...]
            o_ref[...] = _pack_bf16_pairs(y)
            issue(nbase, ISSUE_SPLIT[1], MOE_ROWS - ISSUE_SPLIT[1], nxt, nxt_sem)

        @pl.when(i == n_used - 1)
        def _():
            drain(nxt, nxt_sem)

    @pl.when(jnp.logical_and(i < n_used, i % 2 == 0))
    def _():
        step(xbuf0, sem.at[0], xbuf1, sem.at[1])

    @pl.when(jnp.logical_and(i < n_used, i % 2 == 1))
    def _():
        step(xbuf1, sem.at[1], xbuf0, sem.at[0])

    @pl.when(i >= n_used)
    def _():
        o_ref[...] = jnp.zeros_like(o_ref)


def _experts(h, block_e, n_used, slot_tok, first, next_e, has_next, w_gu, b_gu, w_down, b_down, l):
    n_slots = slot_tok.shape[0]
    d = D_MODEL
    bias = lambda n: pl.BlockSpec((None, None, 1, n), lambda i, be, *_: (l, be[i], 0, 0))
    grid_spec = pltpu.PrefetchScalarGridSpec(
        num_scalar_prefetch=6,
        grid=(n_slots // MOE_ROWS,),
        in_specs=[pl.BlockSpec(memory_space=pl.ANY), pl.BlockSpec(memory_space=pl.ANY), bias(2 * D_EXPERT),
                  pl.BlockSpec(memory_space=pl.ANY), bias(d)],
        out_specs=pl.BlockSpec((MOE_ROWS, HALF_D), lambda i, *_: (i, 0)),
        scratch_shapes=[pltpu.VMEM((MOE_ROWS, HALF_D), jnp.uint32), pltpu.VMEM((MOE_ROWS, HALF_D), jnp.uint32),
                        pltpu.VMEM((MOE_ROWS, d), BF16), pltpu.VMEM((MOE_ROWS, D_EXPERT), BF16),
                        pltpu.VMEM((d, 2 * D_EXPERT), F32), pltpu.VMEM((D_EXPERT, d), F32),
                        pltpu.VMEM((d, 2 * D_EXPERT), BF16), pltpu.VMEM((D_EXPERT, d), BF16),
                        pltpu.SemaphoreType.DMA((2,)), pltpu.SemaphoreType.DMA((2,))],
    )
    return pl.pallas_call(
        functools.partial(_expert_kernel, l),
        grid_spec=grid_spec,
        out_shape=jax.ShapeDtypeStruct((n_slots, HALF_D), jnp.uint32),
        compiler_params=_params("arbitrary"),
        name="experts",
    )(block_e, n_used, slot_tok, first, next_e, has_next, h, w_gu, b_gu, w_down, b_down)


COMBINE_TM = 128


def _combine_kernel(dest_ref, ys_hbm, wt_ref, x_ref, gt_ref, o_ref, buf0, buf1, sem):
    i = pl.program_id(0)
    n = pl.num_programs(0)
    tm = COMBINE_TM

    def issue(tile, buf, s):
        base = tile * (tm * TOP_K)
        for r in range(tm):
            for k in range(TOP_K):
                _row_copy(ys_hbm, dest_ref[base + r * TOP_K + k], buf.at[k], r, s).start()

    def drain(buf, s):
        for r in range(tm):
            for k in range(TOP_K):
                _row_copy(ys_hbm, 0, buf.at[k], r, s).wait()

    @pl.when(i == 0)
    def _():
        issue(0, buf0, sem.at[0])

    def prefetch(cur, cur_sem, nxt, nxt_sem):
        drain(cur, cur_sem)
        issue(jnp.minimum(i + 1, n - 1), nxt, nxt_sem)

    def compute(cur, nxt, nxt_sem):
        acc_lo = acc_hi = None
        for k in range(TOP_K):
            lo, hi = _unpack_bf16_pairs(cur[k])
            w = wt_ref[:, k:k + 1]
            acc_lo = w * lo if acc_lo is None else acc_lo + w * lo
            acc_hi = w * hi if acc_hi is None else acc_hi + w * hi
        o_ref[:, :HALF_D] = x_ref[:, :HALF_D] + gt_ref[:, :HALF_D] * acc_lo
        o_ref[:, HALF_D:] = x_ref[:, HALF_D:] + gt_ref[:, HALF_D:] * acc_hi

        @pl.when(i == n - 1)
        def _():
            drain(nxt, nxt_sem)

    even = i % 2 == 0

    @pl.when(even)
    def _():
        prefetch(buf0, sem.at[0], buf1, sem.at[1])

    @pl.when(jnp.logical_not(even))
    def _():
        prefetch(buf1, sem.at[1], buf0, sem.at[0])

    @pl.when(i % 2 < 1)
    def _():
        compute(buf0, buf1, sem.at[1])

    @pl.when(i % 2 >= 1)
    def _():
        compute(buf1, buf0, sem.at[0])


def _combine(dest, ys, wt, x, mod4, seq):
    m, d = x.shape
    tm = COMBINE_TM
    per_b = seq // tm
    grid_spec = pltpu.PrefetchScalarGridSpec(
        num_scalar_prefetch=1,
        grid=(m // tm,),
        in_specs=[pl.BlockSpec(memory_space=pl.ANY),
                  pl.BlockSpec((tm, LANES), lambda i, ds: (i, 0)),
                  pl.BlockSpec((tm, d), lambda i, ds: (i, 0)),
                  pl.BlockSpec((None, None, 1, d), lambda i, ds: (i // per_b, 5, 0, 0))],
        out_specs=pl.BlockSpec((tm, d), lambda i, ds: (i, 0)),
        scratch_shapes=[pltpu.VMEM((TOP_K, tm, HALF_D), jnp.uint32), pltpu.VMEM((TOP_K, tm, HALF_D), jnp.uint32),
                        pltpu.SemaphoreType.DMA((2,))],
    )
    return pl.pallas_call(
        _combine_kernel,
        grid_spec=grid_spec,
        out_shape=jax.ShapeDtypeStruct((m, d), F32),
        compiler_params=_params("arbitrary"),
        name="moe_combine",
    )(dest, ys, wt, x, mod4)


def _moe(x, g, l, mod4, w_router, b_router, w_gu, b_gu, w_down, b_down, seq):
    m = x.shape[0]
    h, idx, wt, rank, cnt = _router(x, g, l, mod4, w_router, b_router, seq)
    counts = cnt[0, :N_EXPERTS]
    padded = ((counts + MOE_ROWS - 1) // MOE_ROWS) * MOE_ROWS
    ends = jnp.cumsum(padded)
    starts = ends - padded
    dest = (starts[idx[:, :TOP_K]] + rank[:, :TOP_K]).reshape(-1)
    n_slots = m * TOP_K + N_EXPERTS * MOE_ROWS
    n_blocks = n_slots // MOE_ROWS
    slot_tok = jnp.zeros((n_slots,), jnp.int32).at[dest].set(jnp.arange(m * TOP_K, dtype=jnp.int32) // TOP_K)
    block_start = jnp.arange(n_blocks, dtype=jnp.int32) * MOE_ROWS
    block_e = jnp.minimum(jnp.sum(ends[None, :] <= block_start[:, None], axis=1), N_EXPERTS - 1).astype(jnp.int32)
    n_used = (ends[-1:] // MOE_ROWS).astype(jnp.int32)
    first = jnp.concatenate([jnp.ones((1,), jnp.int32), (block_e[1:] != block_e[:-1]).astype(jnp.int32)])
    next_start = (ends // MOE_ROWS)[block_e].astype(jnp.int32)
    has_next = (next_start < n_used[0]).astype(jnp.int32)
    next_e = block_e[jnp.minimum(next_start, n_blocks - 1)]
    ys = _experts(h, block_e, n_used, slot_tok, first, next_e, has_next, w_gu, b_gu, w_down, b_down, l)
    return _combine(dest, ys, wt, x, mod4, seq)


def _pad_lanes(v):
    return jnp.pad(v, ((0, 0), (0, LANES - v.shape[-1])))[:, None, :]


def kernel(x, c, w_mod, b_mod, g_norm_mix, g_norm_ffn, w_in, hg_lower_bound, hg_onorm, s5_lambda_re, s5_lambda_im, s5_log_dt, s5_b_re, s5_b_im, s5_c_re, s5_c_im, s5_d, s5_w_glu, s5_b_glu, m2_conv_w, m2_conv_b, m2_dt_bias, m2_a_log, m2_d, m2_norm, rk_mu, rk_w0, rk_w2, rk_a0, rk_a2, rk_g2, rk_k_k, rk_k_a, rk_r_k, rk_ln_w, rk_ln_b, w_branch, w_out, w_router, b_router, w_gu, b_gu, w_down, b_down, g_final):
    batch, seq, d = x.shape
    depth = w_in.shape[0]
    n = batch * seq
    dt0 = N_MAIN
    w_tail = jnp.concatenate([w_in[:, :, dt0 + M2_HEADS:], w_in[:, :, dt0:dt0 + M2_HEADS],
                              jnp.zeros((depth, d, LANES - M2_HEADS), F32)], axis=-1)
    w_branch_b = w_branch.astype(BF16)
    w_out_b = w_out.astype(BF16)
    w_router_p = jnp.pad(w_router, ((0, 0), (0, 0), (0, LANES - N_EXPERTS)))
    b_router_p = _pad_lanes(b_router)
    b_gu4 = b_gu[:, :, None, :]
    b_down4 = b_down[:, :, None, :]
    lbs = jax.nn.softmax(hg_lower_bound.astype(F32), axis=0)
    lbs = jnp.cumsum(lbs, axis=0) - lbs[0]
    dt_bias_p = _pad_lanes(m2_dt_bias)
    a_log_p = _pad_lanes(m2_a_log)
    m2_d_rep = jnp.repeat(m2_d, M2_HEADDIM, axis=-1)[:, None, :]
    c_pad = jnp.pad(c, ((0, 8 - batch), (0, 0)))
    s5_tables = jax.vmap(functools.partial(_s5_tables, n_chunks=seq // S5_T))(
        s5_lambda_re, s5_lambda_im, s5_log_dt, s5_b_re, s5_b_im, s5_c_re, s5_c_im)

    x2 = x.reshape(n, d)
    for l in range(depth):
        mod = _modulation(c_pad, w_mod, b_mod[:, None, :], l)
        mod4 = mod[:batch].reshape(batch, 6, 1, d)
        h = _norm_modulate(x2, g_norm_mix[:, None, :], l, mod4, 0, 1, seq, BF16)
        proj = _matmul(h, w_in, l, N_MAIN, 512, 1024, F32, "in_proj_main")
        tail = _matmul(h, w_tail, l, N_TAIL, 512, 640, F32, "in_proj_tail")
        ya = _hgrn2(proj, lbs[l][None], hg_onorm[l][None], batch, seq)
        yb = _s5(proj, s5_tables, l, s5_d[l][None], s5_w_glu[l].astype(BF16), s5_b_glu[l][None], batch, seq)
        yc = _mamba2(proj, tail, m2_conv_w[l], m2_conv_b[l][None], dt_bias_p[l], a_log_p[l], m2_d_rep[l],
                     m2_norm[l][None], batch, seq)
        yd = _rwkv7(tail, rk_mu[l][None], rk_w0[l][None], rk_w2[l], rk_a0[l][None], rk_a2[l], rk_g2[l],
                    rk_k_k[l][None], rk_k_a[l][None], rk_r_k[l].reshape(1, BRANCH), rk_ln_w[l][None],
                    rk_ln_b[l][None], batch, seq)
        merged = _merge([ya, yb, yc, yd], proj, w_branch_b, l)
        x2 = _out_proj_residual(merged, w_out_b, l, x2, mod4, 2, seq)
        x2 = _moe(x2, g_norm_ffn[:, None, :], l, mod4, w_router_p, b_router_p, w_gu, b_gu4,
                  w_down, b_down4, seq)
    return _final_norm(x2, g_final[None]).reshape(batch, seq, d)
```

```python
import functools
import math

import jax
import jax.numpy as jnp
from jax import lax
from jax.experimental import pallas as pl
from jax.experimental.pallas import tpu as pltpu

F32 = jnp.float32
BF16 = jnp.bfloat16
HI = lax.Precision.HIGHEST

D_MODEL = 2048
CHUNK = 64
BRANCH = 512
HG_HEADS, HG_DK = 4, 128
S5_CH, S5_GROUPS, S5_STATE = 16, 32, 64
S5_T = 8
S5_PACK = 8
M2_HEADS, M2_HEADDIM, M2_GROUPS, M2_STATE, M2_CONV, M2_XBC = 8, 64, 2, 128, 4, 1024
RK_HEADS, RK_HEADSIZE = 8, 64
RK_DECAY_LORA, RK_A_LORA, RK_GATE_LORA, RK_WIDTH = 64, 64, 128, 1792
RK_GN_EPS = 64e-5
GATE_IN = 4 * D_MODEL
N_MAIN = GATE_IN + 4 * BRANCH + BRANCH + BRANCH + M2_XBC
N_TAIL = RK_WIDTH + 128
N_EXPERTS, TOP_K, D_EXPERT = 32, 4, 768
SWIGLU_LIMIT, SWIGLU_ALPHA = 7.0, 1.702
MOE_ROWS = 256
NORM_EPS = 1e-6
LANES = 128
VMEM_LIMIT = 48 * 1024 * 1024


def _dot(a, b, prec=None):
    return jnp.dot(a, b, preferred_element_type=F32, precision=prec)


def _dot_nt(a, b, prec=None):
    return lax.dot_general(a, b, (((1,), (1,)), ((), ())), preferred_element_type=F32, precision=prec)


def _dot_tn(a, b, prec=None):
    return lax.dot_general(a, b, (((0,), (0,)), ((), ())), preferred_element_type=F32, precision=prec)


def _bdot(a, b):
    return _dot(a.astype(BF16), b.astype(BF16))


def _bdot_nt(a, b):
    return _dot_nt(a.astype(BF16), b.astype(BF16))


def _bdot_tn(a, b):
    return _dot_tn(a.astype(BF16), b.astype(BF16))


def _sigmoid(x):
    return 1.0 / (1.0 + jnp.exp(-x))


def _silu(x):
    return x * _sigmoid(x)


def _softplus(x):
    return jnp.maximum(x, 0.0) + jnp.log(1.0 + jnp.exp(-jnp.abs(x)))


def _tril(n, k=0):
    r = lax.broadcasted_iota(jnp.int32, (n, n), 0)
    c = lax.broadcasted_iota(jnp.int32, (n, n), 1)
    return (c - r) <= k


def _params(*sem):
    return pltpu.CompilerParams(dimension_semantics=sem, vmem_limit_bytes=VMEM_LIMIT)


def _mm_kernel(x_ref, w_ref, o_ref, wbf_ref):
    @pl.when(pl.program_id(1) == 0)
    def _():
        wbf_ref[...] = w_ref[...].astype(BF16)

    o_ref[...] = _dot_nt(x_ref[...], wbf_ref[...]).astype(o_ref.dtype)


def _mod_kernel(c_ref, w_ref, b_ref, o_ref):
    o_ref[...] = _dot(_silu(c_ref[...]).astype(BF16), w_ref[...].astype(BF16)) + b_ref[...]


def _mm_resid_kernel(x_ref, w_ref, r_ref, g_ref, o_ref):
    o_ref[...] = r_ref[...] + g_ref[...] * _dot(x_ref[...], w_ref[...])


def _matmul(x, w, l, n_cols, tm, tn, out_dtype, name):
    m, k = x.shape
    return pl.pallas_call(
        _mm_kernel,
        grid=(n_cols // tn, m // tm),
        in_specs=[pl.BlockSpec((tm, k), lambda j, i: (i, 0)),
                  pl.BlockSpec((None, tn, k), lambda j, i: (l, j, 0))],
        out_specs=pl.BlockSpec((tm, tn), lambda j, i: (i, j)),
        out_shape=jax.ShapeDtypeStruct((m, n_cols), out_dtype),
        scratch_shapes=[pltpu.VMEM((tn, k), BF16)],
        compiler_params=_params("parallel", "arbitrary"),
        name=name,
    )(x, w)


def _modulation(cond, w_mod, b_mod, l):
    m, k = cond.shape
    n = w_mod.shape[-1]
    tn = 1024
    return pl.pallas_call(
        _mod_kernel,
        grid=(n // tn,),
        in_specs=[pl.BlockSpec((m, k), lambda j: (0, 0)),
                  pl.BlockSpec((None, k, tn), lambda j: (l, 0, j)),
                  pl.BlockSpec((None, 1, tn), lambda j: (l, 0, j))],
        out_specs=pl.BlockSpec((m, tn), lambda j: (0, j)),
        out_shape=jax.ShapeDtypeStruct((m, n), F32),
        compiler_params=_params("parallel"),
        name="modulation",
    )(cond, w_mod, b_mod)


def _out_proj_residual(merged, w_out, l, x, mod4, gate_idx, seq):
    m, k = merged.shape
    tm, tn = 512, 512
    per_b = seq // tm
    return pl.pallas_call(
        _mm_resid_kernel,
        grid=(D_MODEL // tn, m // tm),
        in_specs=[pl.BlockSpec((tm, k), lambda j, i: (i, 0)),
                  pl.BlockSpec((None, k, tn), lambda j, i: (l, 0, j)),
                  pl.BlockSpec((tm, tn), lambda j, i: (i, j)),
                  pl.BlockSpec((None, None, 1, tn), lambda j, i: (i // per_b, gate_idx, 0, j))],
        out_specs=pl.BlockSpec((tm, tn), lambda j, i: (i, j)),
        out_shape=jax.ShapeDtypeStruct((m, D_MODEL), F32),
        compiler_params=_params("parallel", "parallel"),
        name="out_proj_residual",
    )(merged, w_out, x, mod4)


def _rms(x, g):
    return x * lax.rsqrt(jnp.mean(x * x, axis=-1, keepdims=True) + NORM_EPS) * g


def _normmod_kernel(x_ref, g_ref, sc_ref, sh_ref, o_ref):
    y = _rms(x_ref[...], g_ref[...])
    o_ref[...] = (y * (1.0 + sc_ref[...]) + sh_ref[...]).astype(o_ref.dtype)


def _norm_modulate(x, g, l, mod4, shift_idx, scale_idx, seq, out_dtype):
    m, d = x.shape
    ts = 512
    per_b = seq // ts
    return pl.pallas_call(
        _normmod_kernel,
        grid=(m // ts,),
        in_specs=[pl.BlockSpec((ts, d), lambda i: (i, 0)),
                  pl.BlockSpec((None, 1, d), lambda i: (l, 0, 0)),
                  pl.BlockSpec((None, None, 1, d), lambda i: (i // per_b, scale_idx, 0, 0)),
                  pl.BlockSpec((None, None, 1, d), lambda i: (i // per_b, shift_idx, 0, 0))],
        out_specs=pl.BlockSpec((ts, d), lambda i: (i, 0)),
        out_shape=jax.ShapeDtypeStruct((m, d), out_dtype),
        compiler_params=_params("parallel"),
        name="norm_modulate",
    )(x, g, mod4, mod4)


def _final_norm_kernel(x_ref, g_ref, o_ref):
    o_ref[...] = _rms(x_ref[...], g_ref[...])


def _final_norm(x, g):
    m, d = x.shape
    ts = 512
    return pl.pallas_call(
        _final_norm_kernel,
        grid=(m // ts,),
        in_specs=[pl.BlockSpec((ts, d), lambda i: (i, 0)), pl.BlockSpec((1, d), lambda i: (0, 0))],
        out_specs=pl.BlockSpec((ts, d), lambda i: (i, 0)),
        out_shape=jax.ShapeDtypeStruct((m, d), F32),
        compiler_params=_params("parallel"),
        name="final_norm",
    )(x, g)


HG_SUB = 16


def _hgrn2_kernel(q_ref, f_ref, i_ref, g_ref, lb_ref, on_ref, o_ref, state_ref):
    @pl.when(pl.program_id(1) == 0)
    def _():
        state_ref[...] = jnp.zeros_like(state_ref)

    sub_mask = _tril(HG_SUB)
    lb = lb_ref[...]
    forget = lb + (1.0 - lb) * _sigmoid(f_ref[...])
    k = 1.0 - forget
    q = _silu(q_ref[...])
    v = i_ref[...]
    cum = _dot(_tril(CHUNK).astype(F32), jnp.log(forget), HI)
    last = cum[CHUNK - 1:CHUNK, :]
    q_dec = q * jnp.exp(cum)
    k_end = k * jnp.exp(last - cum)
    heads = range(HG_HEADS)
    sls = [slice(h * HG_DK, (h + 1) * HG_DK) for h in heads]
    sts = [state_ref[h] for h in heads]
    o_inter = [_bdot_nt(q_dec[:, sls[h]], sts[h]) for h in heads]
    for h in heads:
        state_ref[h] = sts[h] * jnp.exp(last[:, sls[h]]) + _bdot_tn(v[:, sls[h]], k_end[:, sls[h]])
    rows = [[None] * (CHUNK // HG_SUB) for _ in heads]
    for i in range(CHUNK // HG_SUB):
        r0 = i * HG_SUB
        if i > 0:
            ci = cum[r0 - 1:r0, :]
            qp = q[r0:r0 + HG_SUB] * jnp.exp(cum[r0:r0 + HG_SUB] - ci)
            kp = k[:r0] * jnp.exp(ci - cum[:r0])
            sc = [_bdot_nt(qp[:, sls[h]], kp[:, sls[h]]) for h in heads]
            off = [_bdot(sc[h], v[:r0, sls[h]]) for h in heads]
        for h in heads:
            sl = sls[h]
            qi = q[r0:r0 + HG_SUB, sl]
            cumi = cum[r0:r0 + HG_SUB, sl]
            oi = o_inter[h][r0:r0 + HG_SUB]
            if i > 0:
                oi = oi + off[h]
            for s in range(HG_SUB):
                diff = jnp.where(sub_mask[:, s:s + 1], cumi - cumi[s:s + 1, :], -jnp.inf)
                w = jnp.sum(qi * jnp.exp(diff) * k[r0 + s:r0 + s + 1, sl], axis=-1, keepdims=True)
                oi = oi + w * v[r0 + s:r0 + s + 1, sl]
            rows[h][i] = oi
    outs = [_rms(jnp.concatenate(rows[h], axis=0), on_ref[...]) for h in heads]
    o_ref[...] = jnp.concatenate(outs, axis=-1) * _silu(g_ref[...])


def _hgrn2(proj, lb, onorm, batch, seq):
    nc = seq // CHUNK
    col0 = GATE_IN // BRANCH
    spec = lambda off: pl.BlockSpec((CHUNK, BRANCH), lambda b, c: (b * nc + c, col0 + off))
    vec = lambda n: pl.BlockSpec((1, n), lambda b, c: (0, 0))
    return pl.pallas_call(
        _hgrn2_kernel,
        grid=(batch, nc),
        in_specs=[spec(0), spec(1), spec(2), spec(3), vec(BRANCH), vec(HG_DK)],
        out_specs=pl.BlockSpec((CHUNK, BRANCH), lambda b, c: (b * nc + c, 0)),
        out_shape=jax.ShapeDtypeStruct((batch * seq, BRANCH), F32),
        scratch_shapes=[pltpu.VMEM((HG_HEADS, HG_DK, HG_DK), F32)],
        compiler_params=_params("parallel", "arbitrary"),
        name="hgrn2",
    )(proj, proj, proj, proj, lb, onorm)


def _mamba2_kernel(z_ref, xbc_ref, dt_ref, cw_ref, cb_ref, dtb_ref, alog_ref, d_ref, ng_ref,
                   o_ref, ext_ref, state_ref):
    @pl.when(pl.program_id(1) == 0)
    def _():
        ext_ref[0:8, :] = jnp.zeros((8, M2_XBC), F32)
        state_ref[...] = jnp.zeros_like(state_ref)

    ext_ref[8:8 + CHUNK, :] = xbc_ref[...]
    conv = cb_ref[...]
    for j in range(M2_CONV):
        conv = conv + cw_ref[j:j + 1, :] * ext_ref[8 - (M2_CONV - 1) + j:8 - (M2_CONV - 1) + j + CHUNK, :]
    ext_ref[0:8, :] = xbc_ref[CHUNK - 8:CHUNK, :]
    xa = _silu(conv)
    gn = M2_GROUPS * M2_STATE
    xs = xa[:, :BRANCH]
    dt = _softplus(dt_ref[...] + dtb_ref[...])
    adt = -jnp.exp(alog_ref[...]) * dt
    tri = _tril(CHUNK)
    cum = _dot(tri.astype(F32), adt, HI)
    cum_t = cum.T
    cb = []
    for g in range(M2_GROUPS):
        bm = xa[:, BRANCH + g * M2_STATE:BRANCH + (g + 1) * M2_STATE]
        cm = xa[:, BRANCH + gn + g * M2_STATE:BRANCH + gn + (g + 1) * M2_STATE]
        cb.append((bm, cm, _bdot_nt(cm, bm)))
    ys = []
    for h in range(M2_HEADS):
        bm, cm, cbg = cb[h // (M2_HEADS // M2_GROUPS)]
        cum_h = cum[:, h:h + 1]
        lmat = jnp.exp(jnp.where(tri, cum_h - cum_t[h:h + 1, :], -jnp.inf))
        x_h = xs[:, h * M2_HEADDIM:(h + 1) * M2_HEADDIM]
        xdt = x_h * dt[:, h:h + 1]
        st = state_ref[h]
        y = _bdot(cbg * lmat, xdt) + jnp.exp(cum_h) * _bdot(cm, st)
        last = cum[CHUNK - 1:CHUNK, h:h + 1]
        state_ref[h] = jnp.exp(last) * st + _bdot_tn(bm * jnp.exp(last - cum_h), xdt)
        ys.append(y)
    y = jnp.concatenate(ys, axis=-1) + d_ref[...] * xs
    y = y * _silu(z_ref[...])
    gw = BRANCH // M2_GROUPS
    outs = [_rms(y[:, g * gw:(g + 1) * gw], ng_ref[:, g * gw:(g + 1) * gw]) for g in range(M2_GROUPS)]
    o_ref[...] = jnp.concatenate(outs, axis=-1)


def _mamba2(proj, tail, conv_w, conv_b, dt_bias, a_log, d_rep, norm_g, batch, seq):
    nc = seq // CHUNK
    zcol = (GATE_IN + 5 * BRANCH) // BRANCH
    xcol = (GATE_IN + 6 * BRANCH) // M2_XBC
    vec = lambda r, n: pl.BlockSpec((r, n), lambda b, c: (0, 0))
    return pl.pallas_call(
        _mamba2_kernel,
        grid=(batch, nc),
        in_specs=[pl.BlockSpec((CHUNK, BRANCH), lambda b, c: (b * nc + c, zcol)),
                  pl.BlockSpec((CHUNK, M2_XBC), lambda b, c: (b * nc + c, xcol)),
                  pl.BlockSpec((CHUNK, LANES), lambda b, c: (b * nc + c, RK_WIDTH // LANES)),
                  vec(M2_CONV, M2_XBC), vec(1, M2_XBC), vec(1, LANES), vec(1, LANES),
                  vec(1, BRANCH), vec(1, BRANCH)],
        out_specs=pl.BlockSpec((CHUNK, BRANCH), lambda b, c: (b * nc + c, 0)),
        out_shape=jax.ShapeDtypeStruct((batch * seq, BRANCH), F32),
        scratch_shapes=[pltpu.VMEM((8 + CHUNK, M2_XBC), F32),
                        pltpu.VMEM((M2_HEADS, M2_STATE, M2_HEADDIM), F32)],
        compiler_params=_params("parallel", "arbitrary"),
        name="mamba2",
    )(proj, proj, tail, conv_w, conv_b, dt_bias, a_log, d_rep, norm_g)


def _unit_lower_inverses(mats):
    n = mats[0].shape[0]
    eye = (lax.broadcasted_iota(jnp.int32, (n, n), 0) == lax.broadcasted_iota(jnp.int32, (n, n), 1)).astype(F32)
    invs = [eye - a for a in mats]
    pows = list(mats)
    k = 2
    while k < n:
        pows = [_bdot(p, p) for p in pows]
        invs = [_bdot(inv, eye + p) for inv, p in zip(invs, pows)]
        k *= 2
    return invs


def _rwkv7_kernel(p_ref, mu_ref, w0_ref, w2_ref, a0_ref, a2_ref, g2_ref, kk_ref, ka_ref, rk_ref,
                  lnw_ref, lnb_ref, o_ref, ext_ref, state_ref):
    @pl.when(pl.program_id(1) == 0)
    def _():
        ext_ref[0:8, :] = jnp.zeros((8, RK_WIDTH), F32)
        state_ref[...] = jnp.zeros_like(state_ref)

    p = p_ref[...]
    ext_ref[8:8 + CHUNK, :] = p
    prev = ext_ref[7:7 + CHUNK, :]
    ext_ref[0:8, :] = p_ref[CHUNK - 8:CHUNK, :]
    pm = p + (prev - p) * mu_ref[...]
    b3 = 3 * BRANCH
    r = pm[:, :BRANCH]
    k = pm[:, BRANCH:2 * BRANCH]
    v = pm[:, 2 * BRANCH:b3]
    w_lo = pm[:, b3:b3 + RK_DECAY_LORA]
    a_lo = pm[:, b3 + RK_DECAY_LORA:b3 + RK_DECAY_LORA + RK_A_LORA]
    g_lo = pm[:, b3 + RK_DECAY_LORA + RK_A_LORA:]
    w = -_softplus(-(w0_ref[...] + _bdot(jnp.tanh(w_lo), w2_ref[...]))) - 0.5
    logw = -jnp.exp(w)
    a = _sigmoid(a0_ref[...] + _bdot(a_lo, a2_ref[...]))
    g = _bdot(_sigmoid(g_lo), g2_ref[...])
    kk = k * kk_ref[...]
    k = k * (1.0 + (a - 1.0) * ka_ref[...])
    c, hs = CHUNK, RK_HEADSIZE
    row = lax.broadcasted_iota(jnp.int32, (2 * c, 2 * c), 0)
    col = lax.broadcasted_iota(jnp.int32, (2 * c, 2 * c), 1)
    quad_mask = (col & (c - 1)) < (row & (c - 1)) + (row >= c).astype(jnp.int32)
    cum = _dot(_tril(c).astype(F32), logw, HI)
    last = cum[c - 1:c, :]
    gam = jnp.exp(cum)
    ginv = jnp.exp(-cum)
    to_end = jnp.exp(last - cum)
    gam_prev = jnp.exp(cum - logw)
    heads = range(RK_HEADS)
    sls = [slice(h * hs, (h + 1) * hs) for h in heads]
    kap = []
    for sl in sls:
        kh = kk[:, sl]
        kap.append(kh / jnp.maximum(jnp.sqrt(jnp.sum(kh * kh, axis=-1, keepdims=True)), 1e-12))
    beta = [kap[h] * a[:, sls[h]] for h in heads]
    lhs = [jnp.concatenate([kap[h] * gam_prev[:, sls[h]], r[:, sls[h]] * gam[:, sls[h]]], axis=0)
           for h in heads]
    rhs = [jnp.concatenate([beta[h] * ginv[:, sls[h]], k[:, sls[h]] * ginv[:, sls[h]]], axis=0)
           for h in heads]
    quad = [jnp.where(quad_mask, _bdot_nt(lhs[h], rhs[h]), 0.0) for h in heads]
    invs = _unit_lower_inverses([q[:c, :c] for q in quad])
    vs = [v[:, sl] for sl in sls]
    akv = [_bdot(quad[h][:, c:], vs[h]) for h in heads]
    sts = [state_ref[h] for h in heads]
    from_state = [_bdot_nt(lhs[h], sts[h]) for h in heads]
    us = [_bdot(invs[h], -(from_state[h][:c] + akv[h][:c])) for h in heads]
    ys = [from_state[h][c:] + akv[h][c:] + _bdot(quad[h][c:, :c], us[h]) for h in heads]
    for h in heads:
        sl = sls[h]
        upd = _bdot_tn(jnp.concatenate([us[h], vs[h]], axis=0),
                       jnp.concatenate([beta[h] * to_end[:, sl], k[:, sl] * to_end[:, sl]], axis=0))
        state_ref[h] = sts[h] * gam[c - 1:c, sl] + upd
    outs = []
    for h in heads:
        sl = sls[h]
        mean = jnp.mean(ys[h], axis=-1, keepdims=True)
        yc = ys[h] - mean
        var = jnp.mean(yc * yc, axis=-1, keepdims=True)
        yn = yc * lax.rsqrt(var + RK_GN_EPS) * lnw_ref[:, sl] + lnb_ref[:, sl]
        bonus = jnp.sum(r[:, sl] * k[:, sl] * rk_ref[:, sl], axis=-1, keepdims=True) * vs[h]
        outs.append(yn + bonus)
    o_ref[...] = jnp.concatenate(outs, axis=-1) * g


def _rwkv7(tail, mu, w0, w2, a0, a2, g2, k_k, k_a, r_k, ln_w, ln_b, batch, seq):
    nc = seq // CHUNK
    vec = lambda r, n: pl.BlockSpec((r, n), lambda b, c: (0, 0))
    return pl.pallas_call(
        _rwkv7_kernel,
        grid=(batch, nc),
        in_specs=[pl.BlockSpec((CHUNK, RK_WIDTH), lambda b, c: (b * nc + c, 0)),
                  vec(1, RK_WIDTH), vec(1, BRANCH), vec(RK_DECAY_LORA, BRANCH), vec(1, BRANCH),
                  vec(RK_A_LORA, BRANCH), vec(RK_GATE_LORA, BRANCH), vec(1, BRANCH), vec(1, BRANCH),
                  vec(1, BRANCH), vec(1, BRANCH), vec(1, BRANCH)],
        out_specs=pl.BlockSpec((CHUNK, BRANCH), lambda b, c: (b * nc + c, 0)),
        out_shape=jax.ShapeDtypeStruct((batch * seq, BRANCH), F32),
        scratch_shapes=[pltpu.VMEM((8 + CHUNK, RK_WIDTH), F32),
                        pltpu.VMEM((RK_HEADS, RK_HEADSIZE, RK_HEADSIZE), F32)],
        compiler_params=_params("parallel", "arbitrary"),
        name="rwkv7",
    )(tail, mu, w0, w2, a0, a2, g2, k_k, k_a, r_k, ln_w, ln_b)


def _s5_tables(lam_re, lam_im, log_dt, b_re, b_im, c_re, c_im, n_chunks):
    t = S5_T
    g, p, i = b_re.shape
    dt = jnp.exp(log_dt)[:, None]

    def apow(n):
        e = n.astype(F32)[..., None, None]
        mag = jnp.exp(e * lam_re * dt)
        return mag * jnp.cos(e * lam_im * dt), mag * jnp.sin(e * lam_im * dt)

    ab_re, ab_im = apow(jnp.ones((), F32))
    den = lam_re * lam_re + lam_im * lam_im
    nr = ab_re - 1.0
    coef_re = ((nr * lam_re + ab_im * lam_im) / den)[..., None]
    coef_im = ((ab_im * lam_re - nr * lam_im) / den)[..., None]
    bt_re = coef_re * b_re - coef_im * b_im
    bt_im = coef_re * b_im + coef_im * b_re
    ar, ai = apow(jnp.arange(t + 1))
    cr, ci = c_re[None], c_im[None]
    ca_re = cr * ar[:, :, None, :] - ci * ai[:, :, None, :]
    ca_im = cr * ai[:, :, None, :] + ci * ar[:, :, None, :]
    kern = (jnp.einsum('tgop,gpi->tgoi', ca_re[:t], bt_re, precision=HI)
            - jnp.einsum('tgop,gpi->tgoi', ca_im[:t], bt_im, precision=HI))
    lag = jnp.arange(t)[None, :] - jnp.arange(t)[:, None]
    toep = jnp.where((lag >= 0)[:, :, None, None, None], kern[jnp.maximum(lag, 0)], 0.0)
    toep = toep.transpose(2, 0, 4, 1, 3).reshape(g, t * i, t * i)
    br, bi = ar[:t][::-1], ai[:t][::-1]
    bs_re = br[..., None] * bt_re - bi[..., None] * bt_im
    bs_im = br[..., None] * bt_im + bi[..., None] * bt_re
    bst = jnp.concatenate([bs_re, bs_im], axis=2).transpose(1, 0, 3, 2).reshape(g, t * i, 2 * p)
    cst = jnp.concatenate([ca_re[1:], -ca_im[1:]], axis=3)
    cst = cst.transpose(1, 3, 0, 2).reshape(g, 2 * p, t * i)
    levels = max(1, int(math.log2(n_chunks)))
    pr, pi = apow(t * (2 ** jnp.arange(levels)))
    m1 = jnp.concatenate([pr, pr], axis=-1)
    m2 = jnp.concatenate([-pi, pi], axis=-1)
    k = S5_PACK
    a = g // k
    eye = jnp.eye(k, dtype=F32)
    w = k * i
    place = (eye[:, None, :, None] * jnp.eye(i, dtype=F32)[None, :, None, :]).reshape(k, i, w)
    toep_p = jnp.einsum('agsizo,goc->asgizc', toep.reshape(a, k, t, i, t, i), place).reshape(a, t * w, t * w)
    bst_p = jnp.einsum('agsip,gh->asgihp', bst.reshape(a, k, t, i, 2 * p), eye).reshape(a, t * w, k * 2 * p)
    cst_p = jnp.einsum('agpzo,goc->agpzc', cst.reshape(a, k, 2 * p, t, i), place).reshape(a, k * 2 * p, t * w)
    m1 = m1.reshape(levels, a, 1, k * 2 * p)
    m2 = m2.reshape(levels, a, 1, k * 2 * p)
    return toep_p.astype(BF16), bst_p.astype(BF16), cst_p.astype(BF16), m1, m2


def _s5_kernel(n_chunks, u_ref, toep_ref, bst_ref, cst_ref, m1_ref, m2_ref, y_ref, u_scr):
    for s in range(S5_T):
        u_scr[:, s * LANES:(s + 1) * LANES] = u_ref[:, s, :].astype(BF16)
    u = u_scr[...]
    y = _dot(u, toep_ref[...])
    x = _dot(u, bst_ref[...])
    rows = lax.broadcasted_iota(jnp.int32, x.shape, 0) & (n_chunks - 1)
    lane = lax.broadcasted_iota(jnp.int32, x.shape, 1)
    re_half = (lane & (2 * S5_STATE - 1)) < S5_STATE
    width = x.shape[1]

    def shifted(val, d):
        return jnp.where(rows >= d, pltpu.roll(val, d, axis=0), 0.0)

    def swap_re_im(val):
        return jnp.where(re_half, pltpu.roll(val, width - S5_STATE, axis=1), pltpu.roll(val, S5_STATE, axis=1))

    d = 1
    lvl = 0
    while d < n_chunks:
        xs = shifted(x, d)
        x = x + xs * m1_ref[lvl] + swap_re_im(xs) * m2_ref[lvl]
        d *= 2
        lvl += 1
    y = y + _dot(shifted(x, 1).astype(BF16), cst_ref[...])
    for s in range(S5_T):
        y_ref[:, s, :] = y[:, s * LANES:(s + 1) * LANES]


def _s5_scan(proj3, toep, bst, cst, m1, m2, l, n_chunks):
    r = proj3.shape[0]
    a, w = toep.shape[1], toep.shape[2]
    levels = m1.shape[1]
    ucol = (GATE_IN + 4 * BRANCH) // LANES
    tab = lambda: pl.BlockSpec((None, None, w, w), lambda i: (l, i, 0, 0))
    vec = lambda: pl.BlockSpec((None, levels, None, 1, w), lambda i: (l, 0, i, 0, 0))
    return pl.pallas_call(
        functools.partial(_s5_kernel, n_chunks),
        grid=(a,),
        in_specs=[pl.BlockSpec((r, S5_T, LANES), lambda i: (0, 0, ucol + i)), tab(), tab(), tab(), vec(), vec()],
        out_specs=pl.BlockSpec((r, S5_T, LANES), lambda i: (0, 0, i)),
        out_shape=jax.ShapeDtypeStruct((r, S5_T, BRANCH), F32),
        scratch_shapes=[pltpu.VMEM((r, w), BF16)],
        compiler_params=_params("parallel"),
        name="s5_scan",
    )(proj3, toep, bst, cst, m1, m2)


def _s5_post_kernel(y_ref, u_ref, d_ref, w_ref, b_ref, o_ref):
    y = y_ref[...] + d_ref[...] * u_ref[...]
    y = 0.5 * y * (1.0 + jnp.tanh(math.sqrt(2.0 / math.pi) * (y + 0.044715 * (y * y * y))))
    o_ref[...] = y * _sigmoid(_bdot(y, w_ref[...]) + b_ref[...])


def _s5_post(y_ssm, proj, d_skip, w_glu, b_glu):
    m = y_ssm.shape[0]
    tm = min(512, m)
    ucol = (GATE_IN + 4 * BRANCH) // BRANCH
    return pl.pallas_call(
        _s5_post_kernel,
        grid=(m // tm,),
        in_specs=[pl.BlockSpec((tm, BRANCH), lambda i: (i, 0)),
                  pl.BlockSpec((tm, BRANCH), lambda i: (i, ucol)),
                  pl.BlockSpec((1, BRANCH), lambda i: (0, 0)),
                  pl.BlockSpec((BRANCH, BRANCH), lambda i: (0, 0)),
                  pl.BlockSpec((1, BRANCH), lambda i: (0, 0))],
        out_specs=pl.BlockSpec((tm, BRANCH), lambda i: (i, 0)),
        out_shape=jax.ShapeDtypeStruct((m, BRANCH), F32),
        compiler_params=_params("parallel"),
        name="s5_post",
    )(y_ssm, proj, d_skip, w_glu, b_glu)


def _s5(proj, tables, l, d_skip, w_glu, b_glu, batch, seq):
    n = batch * seq
    y3 = _s5_scan(proj.reshape(n // S5_T, S5_T, proj.shape[1]), *tables, l, seq // S5_T)
    return _s5_post(y3.reshape(n, BRANCH), proj, d_skip, w_glu, b_glu)


def _merge_kernel(ya_ref, yb_ref, yc_ref, yd_ref, ga_ref, gb_ref, gc_ref, gd_ref,
                  wa_ref, wb_ref, wc_ref, wd_ref, o_ref):
    acc = None
    for y_ref, g_ref, w_ref in ((ya_ref, ga_ref, wa_ref), (yb_ref, gb_ref, wb_ref),
                                (yc_ref, gc_ref, wc_ref), (yd_ref, gd_ref, wd_ref)):
        term = _sigmoid(g_ref[...]) * _dot(y_ref[...].astype(BF16), w_ref[...])
        acc = term if acc is None else acc + term
    o_ref[...] = acc.astype(o_ref.dtype)


def _merge(ys, proj, w_branch, l):
    m = ys[0].shape[0]
    tm = 256
    yspec = pl.BlockSpec((tm, BRANCH), lambda i: (i, 0))
    gspec = lambda n: pl.BlockSpec((tm, D_MODEL), lambda i: (i, n))
    wspec = lambda n: pl.BlockSpec((None, None, BRANCH, D_MODEL), lambda i: (l, n, 0, 0))
    return pl.pallas_call(
        _merge_kernel,
        grid=(m // tm,),
        in_specs=[yspec] * 4 + [gspec(n) for n in range(4)] + [wspec(n) for n in range(4)],
        out_specs=pl.BlockSpec((tm, D_MODEL), lambda i: (i, 0)),
        out_shape=jax.ShapeDtypeStruct((m, D_MODEL), BF16),
        compiler_params=_params("parallel"),
        name="branch_merge",
    )(*ys, proj, proj, proj, proj, w_branch, w_branch, w_branch, w_branch)


ROUTER_TM = 256
HALF_D = D_MODEL // 2


def _pack_bf16_halves(lo, hi):
    lo_bits = pltpu.bitcast(lo.astype(BF16).astype(F32), jnp.uint32)
    hi_bits = pltpu.bitcast(hi.astype(BF16).astype(F32), jnp.uint32)
    return (hi_bits & jnp.uint32(0xFFFF0000)) | (lo_bits >> 16)


def _pack_bf16_pairs(v):
    return _pack_bf16_halves(v[:, :HALF_D], v[:, HALF_D:])


def _unpack_bf16_pairs(p):
    lo = pltpu.bitcast(p << 16, F32)
    hi = pltpu.bitcast(p & jnp.uint32(0xFFFF0000), F32)
    return lo, hi


def _router_kernel(x_ref, g_ref, sc_ref, sh_ref, wr_ref, br_ref,
                   h_ref, idx_ref, wt_ref, rank_ref, cnt_ref, carry_ref):
    @pl.when(pl.program_id(0) == 0)
    def _():
        carry_ref[...] = jnp.zeros_like(carry_ref)

    h = _rms(x_ref[...], g_ref[...]) * (1.0 + sc_ref[...]) + sh_ref[...]
    h_ref[...] = _pack_bf16_pairs(h)
    tm = h.shape[0]
    lane = lax.broadcasted_iota(jnp.int32, (tm, LANES), 1)
    logits = _dot(h, wr_ref[...], HI) + br_ref[...]
    masked = jnp.where(lane < N_EXPERTS, logits, -jnp.inf)
    vals, hots, idxs = [], [], []
    for _ in range(TOP_K):
        m = jnp.max(masked, axis=-1, keepdims=True)
        idx = jnp.min(jnp.where(masked == m, lane, LANES), axis=-1, keepdims=True)
        hot = lane == idx
        masked = jnp.where(hot, -jnp.inf, masked)
        vals.append(m)
        hots.append(hot)
        idxs.append(idx)
    exps = [jnp.exp(v - vals[0]) for v in vals]
    denom = exps[0] + exps[1] + exps[2] + exps[3]
    multi = sum(hot.astype(F32) for hot in hots)
    before = _dot(_tril(tm, -1).astype(BF16), multi.astype(BF16)) + carry_ref[...]
    carry_ref[...] = carry_ref[...] + jnp.sum(multi, axis=0, keepdims=True)
    idx_out = jnp.zeros((tm, LANES), jnp.int32)
    wt_out = jnp.zeros((tm, LANES), F32)
    rank_out = jnp.zeros((tm, LANES), F32)
    for k in range(TOP_K):
        rank = jnp.sum(jnp.where(hots[k], before, 0.0), axis=-1, keepdims=True)
        idx_out = jnp.where(lane == k, idxs[k], idx_out)
        wt_out = jnp.where(lane == k, exps[k] / denom, wt_out)
        rank_out = jnp.where(lane == k, rank, rank_out)
    idx_ref[...] = idx_out
    wt_ref[...] = wt_out
    rank_ref[...] = rank_out.astype(jnp.int32)
    cnt_ref[...] = carry_ref[...].astype(jnp.int32)


def _router(x, g, l, mod4, w_router, b_router, seq):
    m, d = x.shape
    tm = min(ROUTER_TM, m)
    per_b = seq // tm
    tok = lambda dt: jax.ShapeDtypeStruct((m, LANES), dt)
    tspec = pl.BlockSpec((tm, LANES), lambda i: (i, 0))
    return pl.pallas_call(
        _router_kernel,
        grid=(m // tm,),
        in_specs=[pl.BlockSpec((tm, d), lambda i: (i, 0)),
                  pl.BlockSpec((None, 1, d), lambda i: (l, 0, 0)),
                  pl.BlockSpec((None, None, 1, d), lambda i: (i // per_b, 4, 0, 0)),
                  pl.BlockSpec((None, None, 1, d), lambda i: (i // per_b, 3, 0, 0)),
                  pl.BlockSpec((None, d, LANES), lambda i: (l, 0, 0)),
                  pl.BlockSpec((None, 1, LANES), lambda i: (l, 0, 0))],
        out_specs=[pl.BlockSpec((tm, HALF_D), lambda i: (i, 0)), tspec, tspec, tspec,
                   pl.BlockSpec((1, LANES), lambda i: (0, 0))],
        out_shape=[jax.ShapeDtypeStruct((m, HALF_D), jnp.uint32), tok(jnp.int32), tok(F32), tok(jnp.int32),
                   jax.ShapeDtypeStruct((1, LANES), jnp.int32)],
        scratch_shapes=[pltpu.VMEM((1, LANES), F32)],
        compiler_params=_params("arbitrary"),
        name="router",
    )(x, g, mod4, mod4, w_router, b_router)


def _row_copy(src_hbm, row, buf, r, sem):
    return pltpu.make_async_copy(src_hbm.at[pl.ds(row, 1), :], buf.at[pl.ds(r, 1), :], sem)


ISSUE_SPLIT = (64, 160)


def _expert_kernel(layer, be_ref, nu_ref, st_ref, first_ref, nexte_ref, hasnext_ref,
                   h_hbm, wgu_hbm, bgu_ref, wd_hbm, bd_ref, o_ref,
                   xbuf0, xbuf1, x_ref, act_ref, wgu_stage, wd_stage, wgu_ref, wd_ref, sem, wsem):
    i = pl.program_id(0)
    n_used = nu_ref[0]

    def weight_copies(e):
        return (pltpu.make_async_copy(wgu_hbm.at[layer, e], wgu_stage, wsem.at[0]),
                pltpu.make_async_copy(wd_hbm.at[layer, e], wd_stage, wsem.at[1]))

    @pl.when(jnp.logical_and(i == 0, n_used > 0))
    def _():
        for cp in weight_copies(be_ref[0]):
            cp.start()

    @pl.when(jnp.logical_and(i < n_used, first_ref[i] == 1))
    def _():
        for cp in weight_copies(be_ref[i]):
            cp.wait()
        wgu_ref[...] = wgu_stage[...].astype(BF16)
        wd_ref[...] = wd_stage[...].astype(BF16)

        @pl.when(hasnext_ref[i] == 1)
        def _():
            for cp in weight_copies(nexte_ref[i]):
                cp.start()

    def issue(base, r0, count, buf, s):
        for r in range(count):
            _row_copy(h_hbm, st_ref[base + r0 + r], buf, r0 + r, s).start()

    def drain(buf, s):
        for r in range(MOE_ROWS):
            _row_copy(h_hbm, 0, buf, r, s).wait()

    @pl.when(jnp.logical_and(i == 0, n_used > 0))
    def _():
        issue(0, 0, MOE_ROWS, xbuf0, sem.at[0])

    def step(cur, cur_sem, nxt, nxt_sem):
        drain(cur, cur_sem)
        lo, hi = _unpack_bf16_pairs(cur[...])
        x_ref[:, :HALF_D] = lo.astype(BF16)
        x_ref[:, HALF_D:] = hi.astype(BF16)
        nbase = jnp.minimum(i + 1, n_used - 1) * MOE_ROWS
        issue(nbase, 0, ISSUE_SPLIT[0], nxt, nxt_sem)

        @pl.when(n_used > i)
        def _():
            gu = _dot(x_ref[...], wgu_ref[...]) + bgu_ref[...]
            gate = jnp.minimum(gu[:, :D_EXPERT], SWIGLU_LIMIT)
            up = jnp.clip(gu[:, D_EXPERT:], -SWIGLU_LIMIT, SWIGLU_LIMIT)
            act_ref[...] = ((up + 1.0) * gate * _sigmoid(SWIGLU_ALPHA * gate)).astype(BF16)
            issue(nbase, ISSUE_SPLIT[0], ISSUE_SPLIT[1] - ISSUE_SPLIT[0], nxt, nxt_sem)

        @pl.when(n_used >= i + 1)
        def _():
            y = _dot(act_ref[...], wd_ref[...]) + bd_ref[...]
            o_ref[...] = _pack_bf16_pairs(y)
            issue(nbase, ISSUE_SPLIT[1], MOE_ROWS - ISSUE_SPLIT[1], nxt, nxt_sem)

        @pl.when(i == n_used - 1)
        def _():
            drain(nxt, nxt_sem)

    @pl.when(jnp.logical_and(i < n_used, i % 2 == 0))
    def _():
        step(xbuf0, sem.at[0], xbuf1, sem.at[1])

    @pl.when(jnp.logical_and(i < n_used, i % 2 == 1))
    def _():
        step(xbuf1, sem.at[1], xbuf0, sem.at[0])

    @pl.when(i >= n_used)
    def _():
        o_ref[...] = jnp.zeros_like(o_ref)


def _experts(h, block_e, n_used, slot_tok, first, next_e, has_next, w_gu, b_gu, w_down, b_down, l):
    n_slots = slot_tok.shape[0]
    d = D_MODEL
    bias = lambda n: pl.BlockSpec((None, None, 1, n), lambda i, be, *_: (l, be[i], 0, 0))
    grid_spec = pltpu.PrefetchScalarGridSpec(
        num_scalar_prefetch=6,
        grid=(n_slots // MOE_ROWS,),
        in_specs=[pl.BlockSpec(memory_space=pl.ANY), pl.BlockSpec(memory_space=pl.ANY), bias(2 * D_EXPERT),
                  pl.BlockSpec(memory_space=pl.ANY), bias(d)],
        out_specs=pl.BlockSpec((MOE_ROWS, HALF_D), lambda i, *_: (i, 0)),
        scratch_shapes=[pltpu.VMEM((MOE_ROWS, HALF_D), jnp.uint32), pltpu.VMEM((MOE_ROWS, HALF_D), jnp.uint32),
                        pltpu.VMEM((MOE_ROWS, d), BF16), pltpu.VMEM((MOE_ROWS, D_EXPERT), BF16),
                        pltpu.VMEM((d, 2 * D_EXPERT), F32), pltpu.VMEM((D_EXPERT, d), F32),
                        pltpu.VMEM((d, 2 * D_EXPERT), BF16), pltpu.VMEM((D_EXPERT, d), BF16),
                        pltpu.SemaphoreType.DMA((2,)), pltpu.SemaphoreType.DMA((2,))],
    )
    return pl.pallas_call(
        functools.partial(_expert_kernel, l),
        grid_spec=grid_spec,
        out_shape=jax.ShapeDtypeStruct((n_slots, HALF_D), jnp.uint32),
        compiler_params=_params("arbitrary"),
        name="experts",
    )(block_e, n_used, slot_tok, first, next_e, has_next, h, w_gu, b_gu, w_down, b_down)


COMBINE_TM = 128


def _combine_kernel(dest_ref, ys_hbm, wt_ref, x_ref, gt_ref, o_ref, buf0, buf1, sem):
    i = pl.program_id(0)
    n = pl.num_programs(0)
    tm = COMBINE_TM

    def issue(tile, buf, s):
        base = tile * (tm * TOP_K)
        for r in range(tm):
            for k in range(TOP_K):
                _row_copy(ys_hbm, dest_ref[base + r * TOP_K + k], buf.at[k], r, s).start()

    def drain(buf, s):
        for r in range(tm):
            for k in range(TOP_K):
                _row_copy(ys_hbm, 0, buf.at[k], r, s).wait()

    @pl.when(i == 0)
    def _():
        issue(0, buf0, sem.at[0])

    def prefetch(cur, cur_sem, nxt, nxt_sem):
        drain(cur, cur_sem)
        issue(jnp.minimum(i + 1, n - 1), nxt, nxt_sem)

    def compute(cur, nxt, nxt_sem):
        acc_lo = acc_hi = None
        for k in range(TOP_K):
            lo, hi = _unpack_bf16_pairs(cur[k])
            w = wt_ref[:, k:k + 1]
            acc_lo = w * lo if acc_lo is None else acc_lo + w * lo
            acc_hi = w * hi if acc_hi is None else acc_hi + w * hi
        o_ref[:, :HALF_D] = x_ref[:, :HALF_D] + gt_ref[:, :HALF_D] * acc_lo
        o_ref[:, HALF_D:] = x_ref[:, HALF_D:] + gt_ref[:, HALF_D:] * acc_hi

        @pl.when(i == n - 1)
        def _():
            drain(nxt, nxt_sem)

    even = i % 2 == 0

    @pl.when(even)
    def _():
        prefetch(buf0, sem.at[0], buf1, sem.at[1])

    @pl.when(jnp.logical_not(even))
    def _():
        prefetch(buf1, sem.at[1], buf0, sem.at[0])

    @pl.when(i % 2 < 1)
    def _():
        compute(buf0, buf1, sem.at[1])

    @pl.when(i % 2 >= 1)
    def _():
        compute(buf1, buf0, sem.at[0])


def _combine(dest, ys, wt, x, mod4, seq):
    m, d = x.shape
    tm = COMBINE_TM
    per_b = seq // tm
    grid_spec = pltpu.PrefetchScalarGridSpec(
        num_scalar_prefetch=1,
        grid=(m // tm,),
        in_specs=[pl.BlockSpec(memory_space=pl.ANY),
                  pl.BlockSpec((tm, LANES), lambda i, ds: (i, 0)),
                  pl.BlockSpec((tm, d), lambda i, ds: (i, 0)),
                  pl.BlockSpec((None, None, 1, d), lambda i, ds: (i // per_b, 5, 0, 0))],
        out_specs=pl.BlockSpec((tm, d), lambda i, ds: (i, 0)),
        scratch_shapes=[pltpu.VMEM((TOP_K, tm, HALF_D), jnp.uint32), pltpu.VMEM((TOP_K, tm, HALF_D), jnp.uint32),
                        pltpu.SemaphoreType.DMA((2,))],
    )
    return pl.pallas_call(
        _combine_kernel,
        grid_spec=grid_spec,
        out_shape=jax.ShapeDtypeStruct((m, d), F32),
        compiler_params=_params("arbitrary"),
        name="moe_combine",
    )(dest, ys, wt, x, mod4)


def _moe(x, g, l, mod4, w_router, b_router, w_gu, b_gu, w_down, b_down, seq):
    m = x.shape[0]
    h, idx, wt, rank, cnt = _router(x, g, l, mod4, w_router, b_router, seq)
    counts = cnt[0, :N_EXPERTS]
    padded = ((counts + MOE_ROWS - 1) // MOE_ROWS) * MOE_ROWS
    ends = jnp.cumsum(padded)
    starts = ends - padded
    dest = (starts[idx[:, :TOP_K]] + rank[:, :TOP_K]).reshape(-1)
    n_slots = m * TOP_K + N_EXPERTS * MOE_ROWS
    n_blocks = n_slots // MOE_ROWS
    slot_tok = jnp.zeros((n_slots,), jnp.int32).at[dest].set(jnp.arange(m * TOP_K, dtype=jnp.int32) // TOP_K)
    block_start = jnp.arange(n_blocks, dtype=jnp.int32) * MOE_ROWS
    block_e = jnp.minimum(jnp.sum(ends[None, :] <= block_start[:, None], axis=1), N_EXPERTS - 1).astype(jnp.int32)
    n_used = (ends[-1:] // MOE_ROWS).astype(jnp.int32)
    first = jnp.concatenate([jnp.ones((1,), jnp.int32), (block_e[1:] != block_e[:-1]).astype(jnp.int32)])
    next_start = (ends // MOE_ROWS)[block_e].astype(jnp.int32)
    has_next = (next_start < n_used[0]).astype(jnp.int32)
    next_e = block_e[jnp.minimum(next_start, n_blocks - 1)]
    ys = _experts(h, block_e, n_used, slot_tok, first, next_e, has_next, w_gu, b_gu, w_down, b_down, l)
    return _combine(dest, ys, wt, x, mod4, seq)


def _pad_lanes(v):
    return jnp.pad(v, ((0, 0), (0, LANES - v.shape[-1])))[:, None, :]


def kernel(x, c, w_mod, b_mod, g_norm_mix, g_norm_ffn, w_in, hg_lower_bound, hg_onorm, s5_lambda_re, s5_lambda_im, s5_log_dt, s5_b_re, s5_b_im, s5_c_re, s5_c_im, s5_d, s5_w_glu, s5_b_glu, m2_conv_w, m2_conv_b, m2_dt_bias, m2_a_log, m2_d, m2_norm, rk_mu, rk_w0, rk_w2, rk_a0, rk_a2, rk_g2, rk_k_k, rk_k_a, rk_r_k, rk_ln_w, rk_ln_b, w_branch, w_out, w_router, b_router, w_gu, b_gu, w_down, b_down, g_final):
    batch, seq, d = x.shape
    depth = w_in.shape[0]
    n = batch * seq
    dt0 = N_MAIN
    w_in_t = jnp.swapaxes(w_in, 1, 2)
    w_tail_t = jnp.concatenate([w_in_t[:, dt0 + M2_HEADS:], w_in_t[:, dt0:dt0 + M2_HEADS],
                                jnp.zeros((depth, LANES - M2_HEADS, d), F32)], axis=1)
    w_branch_b = w_branch.astype(BF16)
    w_out_b = w_out.astype(BF16)
    w_router_p = jnp.pad(w_router, ((0, 0), (0, 0), (0, LANES - N_EXPERTS)))
    b_router_p = _pad_lanes(b_router)
    b_gu4 = b_gu[:, :, None, :]
    b_down4 = b_down[:, :, None, :]
    lbs = jax.nn.softmax(hg_lower_bound.astype(F32), axis=0)
    lbs = jnp.cumsum(lbs, axis=0) - lbs[0]
    dt_bias_p = _pad_lanes(m2_dt_bias)
    a_log_p = _pad_lanes(m2_a_log)
    m2_d_rep = jnp.repeat(m2_d, M2_HEADDIM, axis=-1)[:, None, :]
    c_pad = jnp.pad(c, ((0, 8 - batch), (0, 0)))
    s5_tables = jax.vmap(functools.partial(_s5_tables, n_chunks=seq // S5_T))(
        s5_lambda_re, s5_lambda_im, s5_log_dt, s5_b_re, s5_b_im, s5_c_re, s5_c_im)

    x2 = x.reshape(n, d)
    for l in range(depth):
        mod = _modulation(c_pad, w_mod, b_mod[:, None, :], l)
        mod4 = mod[:batch].reshape(batch, 6, 1, d)
        h = _norm_modulate(x2, g_norm_mix[:, None, :], l, mod4, 0, 1, seq, BF16)
        proj = _matmul(h, w_in_t, l, N_MAIN, 512, 1024, F32, "in_proj_main")
        tail = _matmul(h, w_tail_t, l, N_TAIL, 512, 640, F32, "in_proj_tail")
        ya = _hgrn2(proj, lbs[l][None], hg_onorm[l][None], batch, seq)
        yb = _s5(proj, s5_tables, l, s5_d[l][None], s5_w_glu[l].astype(BF16), s5_b_glu[l][None], batch, seq)
        yc = _mamba2(proj, tail, m2_conv_w[l], m2_conv_b[l][None], dt_bias_p[l], a_log_p[l], m2_d_rep[l],
                     m2_norm[l][None], batch, seq)
        yd = _rwkv7(tail, rk_mu[l][None], rk_w0[l][None], rk_w2[l], rk_a0[l][None], rk_a2[l], rk_g2[l],
                    rk_k_k[l][None], rk_k_a[l][None], rk_r_k[l].reshape(1, BRANCH), rk_ln_w[l][None],
                    rk_ln_b[l][None], batch, seq)
        merged = _merge([ya, yb, yc, yd], proj, w_branch_b, l)
        x2 = _out_proj_residual(merged, w_out_b, l, x2, mod4, 2, seq)
        x2 = _moe(x2, g_norm_ffn[:, None, :], l, mod4, w_router_p, b_router_p, w_gu, b_gu4,
                  w_down, b_down4, seq)
    return _final_norm(x2, g_final[None]).reshape(batch, seq, d)
```
